```python
import math
import jax, jax.numpy as jnp
from jax import lax
import numpy as np

D_MODEL = 4096
BATCH = 2
SEQ = 4096
DEPTH = 2

GRID_W = 64
CTX_LEN = 256
N_MIXERS = 2
N_HEADS = 32
HEAD_DIM = D_MODEL // N_HEADS
NA_KH_MAX = 8
NA_KW = 16
POOL_WINDOWS = (2, 4, 8, 16)
N_POOL_GROUPS = len(POOL_WINDOWS)
POOL_DG = D_MODEL // N_POOL_GROUPS
N_EXPERTS = 16
EC_CAPACITY_FACTOR = 2
D_FF_EXPERT = (3 * D_MODEL) // 8
N_ADA = 6
N_ATTN_LAYERS = (DEPTH + N_MIXERS - 1) // N_MIXERS
N_POOL_LAYERS = DEPTH // N_MIXERS
NORM_EPS = 1e-6
NEG_INF = -1e30

kernel_name = "hybrid_natten_pool_ecmoe_dit"


def rmsnorm(x, g):
    xf = x.astype(jnp.float32)
    y = xf * lax.rsqrt(jnp.mean(xf * xf, axis=-1, keepdims=True) + NORM_EPS)
    return (y * g.astype(jnp.float32)).astype(x.dtype)


def modulate(h, shift, scale):
    return h * (1 + scale) + shift


def ada_mods(cond, w, b):
    m = jnp.einsum('...d,de->...e', jax.nn.silu(cond), w) + b
    return jnp.split(m, N_ADA, axis=-1)


def qkv_heads(h, w_qkv, q_gain, k_gain):
    B, N, _ = h.shape
    qkv = jnp.einsum('bnd,de->bne', h, w_qkv).reshape(B, N, 3, N_HEADS, HEAD_DIM)
    q = rmsnorm(qkv[:, :, 0], q_gain)
    k = rmsnorm(qkv[:, :, 1], k_gain)
    v = qkv[:, :, 2]
    return q, k, v


def neighbourhood_attention(h_lat, h_ctx, w_qkv, q_gain, k_gain, rpb, w_out, with_ctx_queries):
    B, N, D = h_lat.shape
    n_ctx = h_ctx.shape[1]
    rows = N // GRID_W
    kh = min(NA_KH_MAX, rows)
    kw = min(NA_KW, GRID_W)
    scale = 1.0 / math.sqrt(HEAD_DIM)
    q, k, v = qkv_heads(h_lat, w_qkv, q_gain, k_gain)
    qc, kc, vc = qkv_heads(h_ctx, w_qkv, q_gain, k_gain)
    q_grid = q.reshape(B, rows, GRID_W, N_HEADS, HEAD_DIM)
    k_grid = k.reshape(B, rows, GRID_W, N_HEADS, HEAD_DIM)
    v_grid = v.reshape(B, rows, GRID_W, N_HEADS, HEAD_DIM)

    cols = jnp.arange(GRID_W)
    col_start = jnp.clip(cols - kw // 2, 0, GRID_W - kw)
    col_valid = (cols[None, :] >= col_start[:, None]) & (cols[None, :] < col_start[:, None] + kw)
    dc_idx = jnp.clip(cols[None, :] - cols[:, None] + (NA_KW - 1), 0, 2 * NA_KW - 2)

    def row_block(r):
        q_r = lax.dynamic_index_in_dim(q_grid, r, axis=1, keepdims=False)
        r0 = jnp.clip(r - kh // 2, 0, rows - kh)
        k_blk = lax.dynamic_slice_in_dim(k_grid, r0, kh, axis=1)
        v_blk = lax.dynamic_slice_in_dim(v_grid, r0, kh, axis=1)
        dr_idx = r0 + jnp.arange(kh) - r + (NA_KH_MAX - 1)
        bias = rpb[:, dr_idx[None, :, None], dc_idx[:, None, :]]
        s_lat = jnp.einsum('bqhd,bjkhd->bhqjk', q_r, k_blk).astype(jnp.float32) * scale
        s_lat = s_lat + bias.astype(jnp.float32)[None]
        s_lat = jnp.where(col_valid[:, None, :], s_lat, NEG_INF)
        s_ctx = jnp.einsum('bqhd,bchd->bhqc', q_r, kc).astype(jnp.float32) * scale
        s = jnp.concatenate([s_ctx, s_lat.reshape(B, N_HEADS, GRID_W, kh * GRID_W)], axis=-1)
        p = jax.nn.softmax(s, axis=-1).astype(v.dtype)
        o = jnp.einsum('bhqc,bchd->bqhd', p[..., :n_ctx], vc)
        o = o + jnp.einsum('bhqm,bmhd->bqhd', p[..., n_ctx:],
                           v_blk.reshape(B, kh * GRID_W, N_HEADS, HEAD_DIM))
        return o

    o = lax.map(row_block, jnp.arange(rows))
    o = jnp.moveaxis(o, 0, 1).reshape(B, N, D)
    out_lat = jnp.einsum('bnd,de->bne', o, w_out)
    out_ctx = None
    if with_ctx_queries:
        s = jnp.einsum('bqhd,bkhd->bhqk', qc, kc).astype(jnp.float32) * scale
        p = jax.nn.softmax(s, axis=-1).astype(vc.dtype)
        oc = jnp.einsum('bhqk,bkhd->bqhd', p, vc).reshape(B, n_ctx, D)
        out_ctx = jnp.einsum('bnd,de->bne', oc, w_out)
    return out_lat, out_ctx


def multiscale_pool(h, pool_w, pool_scale):
    B, N, D = h.shape
    hf = h.reshape(B, N, N_POOL_GROUPS, POOL_DG).astype(jnp.float32)
    csum = jnp.concatenate([jnp.zeros_like(hf[:, :1]), lax.cumsum(hf, axis=1)], axis=1)
    t = jnp.arange(N)[:, None]
    half = jnp.array(POOL_WINDOWS, jnp.int32)[None, :] // 2
    lo = jnp.clip(t - half, 0, N)
    hi = jnp.clip(t + half, 0, N)
    g_idx = jnp.arange(N_POOL_GROUPS)[None, :]
    win_sum = csum[:, hi, g_idx] - csum[:, lo, g_idx]
    count = (hi - lo).astype(jnp.float32)[None, :, :, None]
    pooled = (win_sum / count - hf).astype(h.dtype)
    y = jnp.einsum('bngc,gcd->bngd', pooled, pool_w).reshape(B, N, D)
    return y * pool_scale


def expert_choice_moe(h, w_router, w_gate, w_up, w_down):
    B, N, D = h.shape
    cap = max(1, EC_CAPACITY_FACTOR * N // N_EXPERTS)
    aff = jax.nn.softmax(jnp.einsum('bnd,de->bne', h, w_router).astype(jnp.float32), axis=-1)
    gates, idx = lax.top_k(jnp.swapaxes(aff, 1, 2), cap)
    b_idx = jnp.arange(B)[:, None, None]
    xs = h[b_idx, idx]
    hid = jax.nn.silu(jnp.einsum('becd,edf->becf', xs, w_gate)) * jnp.einsum('becd,edf->becf', xs, w_up)
    y = jnp.einsum('becf,efd->becd', hid, w_down) * gates[..., None].astype(h.dtype)
    return jnp.zeros_like(h).at[b_idx, idx].add(y)


def setup_inputs(seed: int = 0) -> dict:
    key = jax.random.key(seed)
    ks = jax.random.split(key, 20)

    def nrm(k, shape, s):
        return jax.random.normal(k, shape, jnp.float32) * s

    D = D_MODEL
    return {
        "x": nrm(ks[0], (BATCH, SEQ, D), 1.0),
        "c": nrm(ks[1], (BATCH, D), 1.0),
        "ctx": nrm(ks[2], (BATCH, CTX_LEN, D), 1.0),
        "c_ctx": nrm(ks[3], (D,), 1.0),
        "ada_w": nrm(ks[4], (DEPTH, D, N_ADA * D), 0.5 * D ** -0.5),
        "ada_b": nrm(ks[5], (DEPTH, N_ADA * D), 0.01),
        "norm_mix_g": 1.0 + nrm(ks[6], (DEPTH, D), 0.01),
        "norm_ffn_g": 1.0 + nrm(ks[7], (DEPTH, D), 0.01),
        "na_w_qkv": nrm(ks[8], (N_ATTN_LAYERS, D, 3 * D), D ** -0.5),
        "na_q_gain": 1.0 + nrm(ks[9], (N_ATTN_LAYERS, HEAD_DIM), 0.01),
        "na_k_gain": 1.0 + nrm(ks[10], (N_ATTN_LAYERS, HEAD_DIM), 0.01),
        "na_rpb": nrm(ks[11], (N_ATTN_LAYERS, N_HEADS, 2 * NA_KH_MAX - 1, 2 * NA_KW - 1), 0.1),
        "na_w_out": nrm(ks[12], (N_ATTN_LAYERS, D, D), D ** -0.5),
        "pool_w": nrm(ks[13], (N_POOL_LAYERS, N_POOL_GROUPS, POOL_DG, POOL_DG), POOL_DG ** -0.5),
        "pool_scale": 1.0 + nrm(ks[14], (N_POOL_LAYERS, D), 0.01),
        "moe_w_router": nrm(ks[15], (DEPTH, D, N_EXPERTS), D ** -0.5),
        "moe_w_gate": nrm(ks[16], (DEPTH, N_EXPERTS, D, D_FF_EXPERT), D ** -0.5),
        "moe_w_up": nrm(ks[17], (DEPTH, N_EXPERTS, D, D_FF_EXPERT), D ** -0.5),
        "moe_w_down": nrm(ks[18], (DEPTH, N_EXPERTS, D_FF_EXPERT, D), D_FF_EXPERT ** -0.5),
    }


def reference(x, c, ctx, c_ctx, ada_w, ada_b, norm_mix_g, norm_ffn_g, na_w_qkv, na_q_gain,
              na_k_gain, na_rpb, na_w_out, pool_w, pool_scale, moe_w_router, moe_w_gate,
              moe_w_up, moe_w_down):
    ctx_s = ctx
    for i in range(DEPTH):
        mixer = i % N_MIXERS
        slot = i // N_MIXERS
        ctx_needed_later = any(j % N_MIXERS == 0 for j in range(i + 1, DEPTH))
        sh_m, sc_m, g_m, sh_f, sc_f, g_f = ada_mods(c[:, None, :], ada_w[i], ada_b[i])
        h = modulate(rmsnorm(x, norm_mix_g[i]), sh_m, sc_m)
        use_ctx = (mixer == 0) or ctx_needed_later
        if use_ctx:
            csh_m, csc_m, cg_m, csh_f, csc_f, cg_f = ada_mods(c_ctx[None, None, :], ada_w[i], ada_b[i])
            hc = modulate(rmsnorm(ctx_s, norm_mix_g[i]), csh_m, csc_m)
        if mixer == 0:
            y, yc = neighbourhood_attention(h, hc, na_w_qkv[slot], na_q_gain[slot], na_k_gain[slot],
                                            na_rpb[slot], na_w_out[slot], ctx_needed_later)
        else:
            y = multiscale_pool(h, pool_w[slot], pool_scale[slot])
            yc = multiscale_pool(hc, pool_w[slot], pool_scale[slot]) if ctx_needed_later else None
        x = x + g_m * y
        hf = modulate(rmsnorm(x, norm_ffn_g[i]), sh_f, sc_f)
        x = x + g_f * expert_choice_moe(hf, moe_w_router[i], moe_w_gate[i], moe_w_up[i], moe_w_down[i])
        if ctx_needed_later:
            ctx_s = ctx_s + cg_m * yc
            hcf = modulate(rmsnorm(ctx_s, norm_ffn_g[i]), csh_f, csc_f)
            ctx_s = ctx_s + cg_f * expert_choice_moe(hcf, moe_w_router[i], moe_w_gate[i],
                                                     moe_w_up[i], moe_w_down[i])
    return x
```

```python
import functools
import math
from typing import NamedTuple

import jax
import jax.numpy as jnp
from jax import lax
from jax.experimental import pallas as pl
from jax.experimental.pallas import tpu as pltpu

F32 = jnp.float32
BF16 = jnp.bfloat16
I32 = jnp.int32

LANE = 128
NORM_EPS = 1e-6
NEG_INF = -1e30
NA_KH = 8
NA_KW = 16
ATTN_QROWS = 4
ATTN_KROWS = ATTN_QROWS + NA_KH
TOK_TILE = 256
COMB_WIN = 64
BF16_ROWS = 16
VMEM_LIMIT = 56 * 1024 * 1024


class Cfg(NamedTuple):
    B: int
    N: int
    D: int
    NC: int
    H: int
    GW: int
    E: int
    CAP: int
    F: int
    G: int
    windows: tuple
    n_ada: int


def _cp(sem, vmem=VMEM_LIMIT):
    return pltpu.CompilerParams(dimension_semantics=sem, vmem_limit_bytes=vmem)


def _dot(a, b):
    return jnp.dot(a, b, preferred_element_type=F32)


def _dot_nt(a, b):
    return lax.dot_general(a, b, (((1,), (1,)), ((), ())), preferred_element_type=F32)


def _silu(x):
    return x * (1.0 / (1.0 + jnp.exp(-x)))


def _ada_kernel(cond_ref, w_ref, b_ref, o_ref):
    s = _silu(cond_ref[...]).astype(BF16)
    o_ref[0] = _dot(s, w_ref[0].astype(BF16)) + b_ref[0]


def ada_mods(cond8, ada_w, ada_b):
    L, D, ND = ada_w.shape
    tn = min(512, ND)
    return pl.pallas_call(
        _ada_kernel,
        grid=(L, ND // tn),
        in_specs=[pl.BlockSpec((8, D), lambda l, j: (0, 0)),
                  pl.BlockSpec((1, D, tn), lambda l, j: (l, 0, j)),
                  pl.BlockSpec((1, 1, tn), lambda l, j: (l, 0, j))],
        out_specs=pl.BlockSpec((1, 8, tn), lambda l, j: (l, 0, j)),
        out_shape=jax.ShapeDtypeStruct((L, 8, ND), F32),
        compiler_params=_cp(("parallel", "parallel")),
        name="ada",
    )(cond8, ada_w, ada_b.reshape(L, 1, ND))


def _norm_mod(x, g, sh, sc):
    ms = jnp.mean(x * x, axis=-1, keepdims=True)
    y = x * lax.rsqrt(ms + NORM_EPS) * g
    return y * (1.0 + sc) + sh


def _normmod_kernel(x_ref, g_ref, sh_ref, sc_ref, o_ref, *, row):
    r = pl.program_id(0) if row is None else row
    sh = sh_ref[0, pl.ds(r, 1), :]
    sc = sc_ref[0, pl.ds(r, 1), :]
    o_ref[0] = _norm_mod(x_ref[0], g_ref[0], sh, sc).astype(o_ref.dtype)


def _mod_spec(D, layer, chunk, ngrid):
    if ngrid == 2:
        return pl.BlockSpec((1, 8, D), lambda b, i: (layer, 0, chunk))
    return pl.BlockSpec((1, 8, D), lambda b, i, j: (layer, 0, chunk))


def normmod(x3, g3, mods, layer, sh_chunk, row, tm):
    B, n, D = x3.shape
    tm = min(tm, n)
    return pl.pallas_call(
        functools.partial(_normmod_kernel, row=row),
        grid=(B, n // tm),
        in_specs=[pl.BlockSpec((1, tm, D), lambda b, i: (b, i, 0)),
                  pl.BlockSpec((1, 1, D), lambda b, i: (layer, 0, 0)),
                  _mod_spec(D, layer, sh_chunk, 2),
                  _mod_spec(D, layer, sh_chunk + 1, 2)],
        out_specs=pl.BlockSpec((1, tm, D), lambda b, i: (b, i, 0)),
        out_shape=jax.ShapeDtypeStruct((B, n, D), BF16),
        compiler_params=_cp(("parallel", "parallel")),
        name="normmod",
    )(x3, g3, mods, mods)


def _qkv_kernel(a_ref, w_ref, gain_ref, o_ref, *, col_off, n_norm, dh):
    acc = _dot(a_ref[...], w_ref[...].astype(BF16))
    j = pl.program_id(1) + col_off
    tn = acc.shape[1]

    @pl.when(j < n_norm)
    def _():
        for c in range(tn // dh):
            blk = acc[:, c * dh:(c + 1) * dh]
            ms = jnp.mean(blk * blk, axis=-1, keepdims=True)
            y = blk * lax.rsqrt(ms + NORM_EPS) * gain_ref[:, c * dh:(c + 1) * dh]
            o_ref[:, c * dh:(c + 1) * dh] = y.astype(o_ref.dtype)

    @pl.when(j >= n_norm)
    def _():
        o_ref[...] = acc.astype(o_ref.dtype)


def qkv_proj(a, w, gain, col0, ncols, dh, norm_cols, tm, tn):
    M, D = a.shape
    tm = min(tm, M)
    off = col0 // tn
    return pl.pallas_call(
        functools.partial(_qkv_kernel, col_off=off, n_norm=norm_cols // tn, dh=dh),
        grid=(M // tm, ncols // tn),
        in_specs=[pl.BlockSpec((tm, D), lambda i, j: (i, 0)),
                  pl.BlockSpec((D, tn), lambda i, j: (0, j + off)),
                  pl.BlockSpec((1, tn), lambda i, j: (0, j + off))],
        out_specs=pl.BlockSpec((tm, tn), lambda i, j: (i, j)),
        out_shape=jax.ShapeDtypeStruct((M, ncols), BF16),
        compiler_params=_cp(("parallel", "arbitrary")),
        name="qkv",
    )(a, w, gain)


def _attn_bias_variants(rows):
    out = []
    for var in range(3):
        tab = {}
        for i in range(ATTN_QROWS):
            for j in range(ATTN_KROWS):
                if var == 0:
                    valid, dr = j < NA_KH, j - i + NA_KH - 1
                elif var == 1:
                    valid, dr = i <= j < i + NA_KH, j - i + NA_KH // 2 - 1
                else:
                    valid, dr = j >= ATTN_KROWS - NA_KH, j - i - 1
                tab[(i, j)] = (valid, dr)
        out.append(tab)
    return out


def _attn_kernel(rpb_ref, q_ref, k_ref, v_ref, kc_ref, vc_ref, o_ref, bias_ref, *, gw, rows):
    nblk = rows // ATTN_QROWS
    qn = ATTN_QROWS * gw
    kn = ATTN_KROWS * gw

    @pl.when(pl.program_id(1) == 0)
    def _build_bias():
        qc = lax.broadcasted_iota(I32, (gw, gw), 0)
        kc = lax.broadcasted_iota(I32, (gw, gw), 1)
        cs = jnp.clip(qc - NA_KW // 2, 0, gw - NA_KW)
        col_mask = jnp.where((kc >= cs) & (kc < cs + NA_KW), 0.0, NEG_INF).astype(F32)
        tiles = []
        for dr in range(2 * NA_KH - 1):
            r = jnp.broadcast_to(rpb_ref[0, dr:dr + 1, :], (gw, LANE))
            t = pltpu.roll(r, LANE - (NA_KW - 1), 1, stride=1, stride_axis=0)
            tiles.append(t[:, :gw] + col_mask)
        neg = jnp.full((gw, gw), NEG_INF, F32)
        for var, tab in enumerate(_attn_bias_variants(rows)):
            for (i, j), (valid, dr) in tab.items():
                bias_ref[var, i * gw:(i + 1) * gw, j * gw:(j + 1) * gw] = tiles[dr] if valid else neg

    kc_all = kc_ref[...]
    vc_all = vc_ref[...]

    def body(blk, carry):
        r0 = blk * ATTN_QROWS
        start = jnp.clip(r0 - NA_KH // 2, 0, rows - ATTN_KROWS)
        var = jnp.where(blk == 0, 0, jnp.where(blk == nblk - 1, 2, 1))
        q0 = pl.multiple_of(r0 * gw, qn)
        k0 = pl.multiple_of(start * gw, gw)
        q = q_ref[pl.ds(q0, qn), :]
        s_lat = _dot_nt(q, k_ref[pl.ds(k0, kn), :]) + bias_ref[var]
        s_ctx = _dot_nt(q, kc_all)
        m = jnp.maximum(jnp.max(s_lat, axis=-1, keepdims=True), jnp.max(s_ctx, axis=-1, keepdims=True))
        p_lat = jnp.exp(s_lat - m)
        p_ctx = jnp.exp(s_ctx - m)
        den = jnp.sum(p_lat, axis=-1, keepdims=True) + jnp.sum(p_ctx, axis=-1, keepdims=True)
        o = _dot(p_lat.astype(BF16), v_ref[pl.ds(k0, kn), :]) + _dot(p_ctx.astype(BF16), vc_all)
        o_ref[pl.ds(q0, qn), :] = (o * (1.0 / den)).astype(o_ref.dtype)
        return carry

    lax.fori_loop(0, nblk, body, 0)


def attention(qkv, kvc, rpb_pad, cfg):
    B, N, D, NC, H, GW = cfg.B, cfg.N, cfg.D, cfg.NC, cfg.H, cfg.GW
    dh = D // H
    rows = N // GW
    assert dh == LANE and rows >= ATTN_KROWS and rows % ATTN_QROWS == 0 and GW >= NA_KW
    return pl.pallas_call(
        functools.partial(_attn_kernel, gw=GW, rows=rows),
        grid=(H, B),
        in_specs=[pl.BlockSpec((1, 2 * NA_KH, LANE), lambda h, b: (h, 0, 0)),
                  pl.BlockSpec((N, dh), lambda h, b: (b, h)),
                  pl.BlockSpec((N, dh), lambda h, b: (b, H + h)),
                  pl.BlockSpec((N, dh), lambda h, b: (b, 2 * H + h)),
                  pl.BlockSpec((NC, dh), lambda h, b: (b, h)),
                  pl.BlockSpec((NC, dh), lambda h, b: (b, H + h))],
        out_specs=pl.BlockSpec((N, dh), lambda h, b: (b, h)),
        out_shape=jax.ShapeDtypeStruct((B * N, D), BF16),
        scratch_shapes=[pltpu.VMEM((3, ATTN_QROWS * GW, ATTN_KROWS * GW), F32)],
        compiler_params=_cp(("arbitrary", "arbitrary")),
        name="attn",
    )(rpb_pad, qkv, qkv, qkv, kvc, kvc)


def _proj_res_kernel(a_ref, w_ref, x_ref, g_ref, *rest, tiles_per_batch, scaled):
    if scaled:
        ps_ref, o_ref = rest
    else:
        (o_ref,) = rest
    y = _dot(a_ref[...], w_ref[0].astype(BF16))
    if scaled:
        y = y * ps_ref[...]
    b = pl.program_id(0) // tiles_per_batch
    o_ref[...] = x_ref[...] + g_ref[0, pl.ds(b, 1), :] * y


def proj_residual(a, w3, x2, mods, layer, gate_chunk, n_per_batch, pscale, tm, tn):
    M = a.shape[0]
    G, K, KO = w3.shape
    Dout = G * KO
    tm = min(tm, n_per_batch)
    tn = min(tn, KO)
    nj = KO // tn
    scaled = pscale is not None
    in_specs = [pl.BlockSpec((tm, K), lambda i, g, j: (i, g)),
                pl.BlockSpec((1, K, tn), lambda i, g, j: (g, 0, j)),
                pl.BlockSpec((tm, tn), lambda i, g, j: (i, g * nj + j)),
                pl.BlockSpec((1, 8, tn), lambda i, g, j: (layer, 0, gate_chunk * (Dout // tn) + g * nj + j))]
    args = [a, w3, x2, mods]
    if scaled:
        in_specs.append(pl.BlockSpec((1, tn), lambda i, g, j: (0, g * nj + j)))
        args.append(pscale)
    return pl.pallas_call(
        functools.partial(_proj_res_kernel, tiles_per_batch=n_per_batch // tm, scaled=scaled),
        grid=(M // tm, G, nj),
        in_specs=in_specs,
        out_specs=pl.BlockSpec((tm, tn), lambda i, g, j: (i, g * nj + j)),
        out_shape=jax.ShapeDtypeStruct((M, Dout), F32),
        compiler_params=_cp(("parallel", "arbitrary", "arbitrary")),
        name="proj_res",
    )(*args)


def _split_bf16(x):
    hi = x.astype(BF16)
    lo = (x - hi.astype(F32)).astype(BF16)
    return hi, lo


def _ffn_norm_kernel(x_ref, g_ref, sh_ref, sc_ref, wr_ref, hf_ref, lg_ref):
    b = pl.program_id(0)
    hf = _norm_mod(x_ref[0], g_ref[0], sh_ref[0, pl.ds(b, 1), :], sc_ref[0, pl.ds(b, 1), :])
    hf_ref[0] = hf
    h_hi, h_lo = _split_bf16(hf)
    w_hi, w_lo = _split_bf16(wr_ref[...])
    lg_ref[0] = _dot(h_hi, w_hi) + (_dot(h_hi, w_lo) + _dot(h_lo, w_hi))


def ffn_norm_router(x3, g3, mods, layer, wr_pad, tm):
    B, N, D = x3.shape
    tm = min(tm, N)
    return pl.pallas_call(
        _ffn_norm_kernel,
        grid=(B, N // tm),
        in_specs=[pl.BlockSpec((1, tm, D), lambda b, i: (b, i, 0)),
                  pl.BlockSpec((1, 1, D), lambda b, i: (layer, 0, 0)),
                  _mod_spec(D, layer, 3, 2),
                  _mod_spec(D, layer, 4, 2),
                  pl.BlockSpec((D, LANE), lambda b, i: (0, 0))],
        out_specs=[pl.BlockSpec((1, tm, D), lambda b, i: (b, i, 0)),
                   pl.BlockSpec((1, tm, LANE), lambda b, i: (b, i, 0))],
        out_shape=[jax.ShapeDtypeStruct((B, N, D), F32),
                   jax.ShapeDtypeStruct((B, N, LANE), F32)],
        compiler_params=_cp(("parallel", "parallel")),
        name="ffn_norm",
    )(x3, g3, mods, mods, wr_pad)


def _route_kernel(lg_ref, idx_ref, gl_ref, posq_ref, starts_ref, aff_ref, *, E, cap, N):
    b = pl.program_id(0)
    nblk = N // TOK_TILE
    lane = lax.broadcasted_iota(I32, (1, LANE), 1)
    lg = jnp.where(lane < E, lg_ref[0], NEG_INF)
    ex = jnp.exp(lg - jnp.max(lg, axis=-1, keepdims=True))
    aff = ex / jnp.sum(ex, axis=-1, keepdims=True)
    aff_ref[...] = aff
    bits = pltpu.bitcast(aff, I32)
    tok = lax.broadcasted_iota(I32, (N, 1), 0)

    def count(pred):
        return jnp.sum(jnp.where(pred, 1, 0).astype(I32), axis=0, keepdims=True)

    def thr_step(i, thr):
        cand = thr | (jnp.int32(1) << (30 - i))
        return jnp.where(count(bits >= cand) >= cap, cand, thr)

    thr = lax.fori_loop(0, 31, thr_step, jnp.zeros((1, LANE), I32))
    gt = bits > thr
    eq = bits == thr
    need = cap - count(gt)

    def tie_step(i, ans):
        cand = ans | (jnp.int32(1) << (int(math.log2(N)) - 1 - i))
        return jnp.where(count(eq & (tok < cand)) < need, cand, ans)

    last = lax.fori_loop(0, int(math.log2(N)), tie_step, jnp.zeros((1, LANE), I32))
    sel = gt | (eq & (tok <= last))

    posq_ref[0] = jnp.where(sel, 1.0, 0.0).astype(F32)
    ri = lax.broadcasted_iota(I32, (TOK_TILE, TOK_TILE), 0)
    ci = lax.broadcasted_iota(I32, (TOK_TILE, TOK_TILE), 1)
    tri = jnp.where(ci < ri, 1.0, 0.0).astype(BF16)
    carry = jnp.zeros((1, LANE), F32)
    starts_ref[0] = jnp.zeros(starts_ref.shape[1:], I32)
    for k in range(nblk):
        sl = pl.ds(k * TOK_TILE, TOK_TILE)
        m = posq_ref[0, sl, :]
        starts_ref[0, k:k + 1, :] = carry.astype(I32)
        pos = _dot(tri, m.astype(BF16)) + carry
        posq_ref[0, sl, :] = jnp.where(m > 0, pos + 1.0, 0.0)
        carry = carry + jnp.sum(m, axis=0, keepdims=True)
    starts_ref[0, nblk:nblk + 1, :] = carry.astype(I32)

    lane_i = lax.broadcasted_iota(I32, (TOK_TILE, LANE), 1)
    for e in range(E):
        for c in range(cap // LANE):
            target = (lane_i + (c * LANE + 1)).astype(F32)

            def blk_step(k, acc):
                sl = pl.ds(pl.multiple_of(k * TOK_TILE, TOK_TILE), TOK_TILE)
                hit = jnp.broadcast_to(posq_ref[0, sl, e:e + 1], (TOK_TILE, LANE)) == target
                tk = lax.broadcasted_iota(I32, (TOK_TILE, LANE), 0) + k * TOK_TILE
                a_i = acc[0] + jnp.sum(jnp.where(hit, tk, 0), axis=0, keepdims=True)
                av = jnp.broadcast_to(aff_ref[sl, e:e + 1], (TOK_TILE, LANE))
                a_g = acc[1] + jnp.sum(jnp.where(hit, av, 0.0), axis=0, keepdims=True)
                return a_i, a_g

            ids, gs = lax.fori_loop(0, nblk, blk_step,
                                    (jnp.zeros((1, LANE), I32), jnp.zeros((1, LANE), F32)))
            idx_ref[0, e:e + 1, c * LANE:(c + 1) * LANE] = ids + b * N
            gl_ref[0, e:e + 1, c * LANE:(c + 1) * LANE] = gs


def route(logits, cfg):
    B, N, E, cap = cfg.B, cfg.N, cfg.E, cfg.CAP
    nblk = N // TOK_TILE
    srows = -(-(nblk + 1) // 8) * 8
    assert N % TOK_TILE == 0 and cap % LANE == 0 and (1 << int(math.log2(N))) == N and E % 8 == 0
    return pl.pallas_call(
        functools.partial(_route_kernel, E=E, cap=cap, N=N),
        grid=(B,),
        in_specs=[pl.BlockSpec((1, N, LANE), lambda b: (b, 0, 0))],
        out_specs=[pl.BlockSpec((1, E, cap), lambda b: (b, 0, 0)),
                   pl.BlockSpec((1, E, cap), lambda b: (b, 0, 0)),
                   pl.BlockSpec((1, N, LANE), lambda b: (b, 0, 0)),
                   pl.BlockSpec((1, srows, LANE), lambda b: (b, 0, 0))],
        out_shape=[jax.ShapeDtypeStruct((B, E, cap), I32),
                   jax.ShapeDtypeStruct((B, E, cap), F32),
                   jax.ShapeDtypeStruct((B, N, LANE), F32),
                   jax.ShapeDtypeStruct((B, srows, LANE), I32)],
        scratch_shapes=[pltpu.VMEM((N, LANE), F32)],
        compiler_params=_cp(("arbitrary",)),
        name="route",
    )(logits)


def _gather_copy(hf_hbm, buf, sem, slot, r, tok):
    return pltpu.make_async_copy(hf_hbm.at[pl.ds(tok, 1), :], buf.at[slot, pl.ds(r, 1), :], sem.at[slot])


def _gather_kernel(idx_ref, hf_hbm, o_ref, buf, sem, *, R):
    i = pl.program_id(0)
    n = pl.num_programs(0)

    def issue(step, slot):
        def body(r, carry):
            _gather_copy(hf_hbm, buf, sem, slot, r, idx_ref[step * R + r]).start()
            return carry
        lax.fori_loop(0, R, body, 0)

    @pl.when(i == 0)
    def _():
        issue(0, 0)

    @pl.when(i + 1 < n)
    def _():
        issue(i + 1, (i + 1) % 2)

    slot = i % 2

    def wait_body(r, carry):
        _gather_copy(hf_hbm, buf, sem, slot, r, 0).wait()
        return carry

    lax.fori_loop(0, R, wait_body, 0)
    o_ref[...] = buf[slot].astype(o_ref.dtype)


def gather_rows(idx_flat, hf2, R):
    M = idx_flat.shape[0]
    D = hf2.shape[1]
    return pl.pallas_call(
        functools.partial(_gather_kernel, R=R),
        grid_spec=pltpu.PrefetchScalarGridSpec(
            num_scalar_prefetch=1,
            grid=(M // R,),
            in_specs=[pl.BlockSpec(memory_space=pl.ANY)],
            out_specs=pl.BlockSpec((R, D), lambda i, idx: (i, 0)),
            scratch_shapes=[pltpu.VMEM((2, R, D), F32), pltpu.SemaphoreType.DMA((2,))]),
        out_shape=jax.ShapeDtypeStruct((M, D), BF16),
        compiler_params=_cp(("arbitrary",)),
        name="gather",
    )(idx_flat, hf2)


def _expert_kernel(xs_ref, wg_ref, wu_ref, wd_ref, gt_ref, o_ref, hid_ref, *, nf, tf):
    s = pl.program_id(1)

    @pl.when(s < nf)
    def _():
        x = xs_ref[...]
        a = _dot(x, wg_ref[0].astype(BF16))
        u = _dot(x, wu_ref[0].astype(BF16))
        hid_ref[s] = (_silu(a) * u).astype(BF16)

    @pl.when(s >= nf)
    def _():
        acc = _dot(hid_ref[0], wd_ref[0, 0:tf, :].astype(BF16))
        for k in range(1, nf):
            acc = acc + _dot(hid_ref[k], wd_ref[0, k * tf:(k + 1) * tf, :].astype(BF16))
        o_ref[...] = (acc * gt_ref[...]).astype(o_ref.dtype)


def experts(xs, w_gate, w_up, w_down, gate_col, tf, tn):
    E, D, F = w_gate.shape
    M = xs.shape[0]
    rows = M // E
    tf = min(tf, F)
    tn = min(tn, D)
    nf, nn = F // tf, D // tn
    return pl.pallas_call(
        functools.partial(_expert_kernel, nf=nf, tf=tf),
        grid=(E, nf + nn),
        in_specs=[pl.BlockSpec((rows, D), lambda e, s: (e, 0)),
                  pl.BlockSpec((1, D, tf), lambda e, s: (e, 0, jnp.minimum(s, nf - 1))),
                  pl.BlockSpec((1, D, tf), lambda e, s: (e, 0, jnp.minimum(s, nf - 1))),
                  pl.BlockSpec((1, F, tn), lambda e, s: (e, 0, jnp.maximum(s - nf, 0))),
                  pl.BlockSpec((rows, 1), lambda e, s: (e, 0))],
        out_specs=pl.BlockSpec((rows, tn), lambda e, s: (e, jnp.maximum(s - nf, 0))),
        out_shape=jax.ShapeDtypeStruct((M, D), BF16),
        scratch_shapes=[pltpu.VMEM((nf, rows, tf), BF16)],
        compiler_params=_cp(("arbitrary", "arbitrary")),
        name="experts",
    )(xs, w_gate, w_up, w_down, gate_col)


def _combine_kernel(st_ref, x_ref, g_ref, posq_ref, yg_hbm, o_ref, ssq_ref, ybuf, acc_ref, sem,
                    *, E, cap, B, nt):
    b = pl.program_id(0)
    t = pl.program_id(1)
    W = COMB_WIN
    KW = E * W
    base = (b * (nt + 1) + t) * E
    lo = [(st_ref[base + e] // BF16_ROWS) * BF16_ROWS for e in range(E)]
    hi = [st_ref[base + E + e] for e in range(E)]
    nrounds = jnp.int32(1)
    for e in range(E):
        nrounds = jnp.maximum(nrounds, (hi[e] - lo[e] + (W - 1)) // W)

    qi = posq_ref[0].astype(I32)
    erow = lax.broadcasted_iota(I32, (LANE, KW), 0)
    ecol = lax.broadcasted_iota(I32, (LANE, KW), 1) // W
    expand = jnp.where(erow == ecol, 1.0, 0.0).astype(BF16)
    qe = (_dot((qi >> 5).astype(F32).astype(BF16), expand) * 32.0
          + _dot((qi & 31).astype(F32).astype(BF16), expand)).astype(I32) - 1
    lane_e = lax.broadcasted_iota(I32, (1, KW), 1) // W
    lane_k = lax.broadcasted_iota(I32, (1, KW), 1) % W

    def window_copy(e, start):
        row0 = pl.multiple_of((e * B + b) * cap + start, BF16_ROWS)
        return pltpu.make_async_copy(yg_hbm.at[pl.ds(row0, W), :], ybuf.at[pl.ds(e * W, W), :], sem.at[0])

    def one_round(r):
        lo_r = [lo[e] + r * W for e in range(E)]
        st = [jnp.minimum(lo_r[e], cap - W) for e in range(E)]
        for e in range(E):
            window_copy(e, st[e]).start()
        stv = jnp.zeros((1, KW), I32)
        lov = jnp.zeros((1, KW), I32)
        for e in range(E):
            stv = jnp.where(lane_e == e, st[e], stv)
            lov = jnp.where(lane_e == e, lo_r[e], lov)
        onehot = (qe - stv == lane_k) & (qe >= lov) & (qe < lov + W)
        s = jnp.where(onehot, 1.0, 0.0).astype(BF16)
        for e in range(E):
            window_copy(e, st[e]).wait()
        return _dot(s, ybuf[...])

    acc_ref[...] = one_round(0)

    def extra(r, carry):
        acc_ref[...] += one_round(r)
        return carry

    lax.fori_loop(1, nrounds, extra, 0)
    x2 = x_ref[...] + g_ref[0, pl.ds(b, 1), :] * acc_ref[...]
    o_ref[...] = x2
    ssq_ref[...] = jnp.sum(x2 * x2, axis=-1, keepdims=True)


def combine(starts_flat, x2d, mods, layer, posq, yg, cfg):
    B, N, D, E, cap = cfg.B, cfg.N, cfg.D, cfg.E, cfg.CAP
    nt = N // TOK_TILE
    assert cap >= COMB_WIN and cap % COMB_WIN == 0 and COMB_WIN % BF16_ROWS == 0 and cap <= 32 * 32
    return pl.pallas_call(
        functools.partial(_combine_kernel, E=E, cap=cap, B=B, nt=nt),
        grid_spec=pltpu.PrefetchScalarGridSpec(
            num_scalar_prefetch=1,
            grid=(B, nt),
            in_specs=[pl.BlockSpec((TOK_TILE, D), lambda b, t, st: (b * nt + t, 0)),
                      pl.BlockSpec((1, 8, D), lambda b, t, st: (layer, 0, 5)),
                      pl.BlockSpec((1, TOK_TILE, LANE), lambda b, t, st: (b, t, 0)),
                      pl.BlockSpec(memory_space=pl.ANY)],
            out_specs=[pl.BlockSpec((TOK_TILE, D), lambda b, t, st: (b * nt + t, 0)),
                       pl.BlockSpec((TOK_TILE, 1), lambda b, t, st: (b * nt + t, 0))],
            scratch_shapes=[pltpu.VMEM((E * COMB_WIN, D), BF16),
                            pltpu.VMEM((TOK_TILE, D), F32),
                            pltpu.SemaphoreType.DMA((1,))]),
        out_shape=[jax.ShapeDtypeStruct((B * N, D), F32),
                   jax.ShapeDtypeStruct((B * N, 1), F32)],
        compiler_params=_cp(("arbitrary", "arbitrary")),
        name="combine",
    )(starts_flat, x2d, mods, posq, yg)


def _pool_kernel(x_ref, ssq_ref, g_ref, sh_ref, sc_ref, o_ref, hs_ref, *, N, D, windows, chunks_per_group):
    b = pl.program_id(0)
    j = pl.program_id(1)
    P = max(windows) // 2
    cw = x_ref.shape[2]
    nblk = N // TOK_TILE
    g = g_ref[0]
    sh = sh_ref[0, pl.ds(b, 1), :]
    sc = sc_ref[0, pl.ds(b, 1), :]
    zeros = jnp.zeros((P, cw), F32)
    hs_ref[0:P, :] = zeros
    hs_ref[P + N:P + N + P, :] = zeros

    def fill(k, carry):
        r0 = pl.multiple_of(k * TOK_TILE, TOK_TILE)
        rstd = lax.rsqrt(ssq_ref[pl.ds(r0, TOK_TILE), :] * (1.0 / D) + NORM_EPS)
        y = x_ref[0, pl.ds(r0, TOK_TILE), :] * rstd * g
        hs_ref[pl.ds(pl.multiple_of(P + r0, 8), TOK_TILE), :] = y * (1.0 + sc) + sh
        return carry

    lax.fori_loop(0, nblk, fill, 0)

    for grp, w in enumerate(windows):
        half = w // 2

        @pl.when(j // chunks_per_group == grp)
        def _(half=half):
            def pool(k, carry):
                r0 = pl.multiple_of(k * TOK_TILE, TOK_TILE)
                big = hs_ref[pl.ds(r0, TOK_TILE + 2 * P), :]
                acc = big[P - half:P - half + TOK_TILE]
                for d in range(-half + 1, half):
                    acc = acc + big[P + d:P + d + TOK_TILE]
                tk = lax.broadcasted_iota(I32, (TOK_TILE, 1), 0) + r0
                cnt = (jnp.minimum(tk + half, N) - jnp.maximum(tk - half, 0)).astype(F32)
                o_ref[0, pl.ds(r0, TOK_TILE), :] = (acc / cnt - big[P:P + TOK_TILE]).astype(o_ref.dtype)
                return carry

            lax.fori_loop(0, nblk, pool, 0)


def pool_mixer_input(x3, ssq, g3, mods, layer, cfg, cw):
    B, N, D = x3.shape
    dg = D // cfg.G
    cw = min(cw, dg)
    P = max(cfg.windows) // 2
    nj = D // cw

    def mspec(chunk):
        return pl.BlockSpec((1, 8, cw), lambda b, j: (layer, 0, chunk * nj + j))

    return pl.pallas_call(
        functools.partial(_pool_kernel, N=N, D=D, windows=cfg.windows, chunks_per_group=dg // cw),
        grid=(B, nj),
        in_specs=[pl.BlockSpec((1, N, cw), lambda b, j: (b, 0, j)),
                  pl.BlockSpec((N, 1), lambda b, j: (b, 0)),
                  pl.BlockSpec((1, 1, cw), lambda b, j: (layer, 0, j)),
                  mspec(0), mspec(1)],
        out_specs=pl.BlockSpec((1, N, cw), lambda b, j: (b, 0, j)),
        out_shape=jax.ShapeDtypeStruct((B, N, D), BF16),
        scratch_shapes=[pltpu.VMEM((N + 2 * P, cw), F32)],
        compiler_params=_cp(("parallel", "parallel")),
        name="pool",
    )(x3, ssq, g3, mods, mods)


def moe_block(x2d, mods, layer, g_ffn3, w_router, w_gate, w_up, w_down, cfg):
    B, N, D, E, cap = cfg.B, cfg.N, cfg.D, cfg.E, cfg.CAP
    nt = N // TOK_TILE
    wr_pad = jnp.pad(w_router, ((0, 0), (0, LANE - E)))
    hf, logits = ffn_norm_router(x2d.reshape(B, N, D), g_ffn3, mods, layer, wr_pad, tm=512)
    idx, gl, posq, starts = route(logits, cfg)
    idx_flat = jnp.transpose(idx, (1, 0, 2)).reshape(-1)
    gate_col = jnp.transpose(gl, (1, 0, 2)).reshape(-1, 1)
    starts_flat = starts[:, :nt + 1, :E].reshape(-1)
    xs = gather_rows(idx_flat, hf.reshape(B * N, D), R=min(256, cap))
    yg = experts(xs, w_gate, w_up, w_down, gate_col, tf=256, tn=512)
    return combine(starts_flat, x2d, mods, layer, posq, yg, cfg)


def forward(cfg, x, c, ctx, c_ctx, ada_w, ada_b, norm_mix_g, norm_ffn_g, na_w_qkv, na_q_gain,
            na_k_gain, na_rpb, na_w_out, pool_w, pool_scale, moe_w_router, moe_w_gate,
            moe_w_up, moe_w_down):
    B, N, D, NC, H = cfg.B, cfg.N, cfg.D, cfg.NC, cfg.H
    dh = D // H
    L = ada_w.shape[0]
    assert L == 2 and B + 1 <= 8
    cond8 = jnp.concatenate([c, c_ctx[None, :], jnp.zeros((8 - B - 1, D), F32)], axis=0)
    mods = ada_mods(cond8, ada_w, ada_b)
    gmix3 = norm_mix_g.reshape(L, 1, D)
    gffn3 = norm_ffn_g.reshape(L, 1, D)

    h = normmod(x, gmix3, mods, 0, 0, None, tm=512).reshape(B * N, D)
    hc = normmod(ctx, gmix3, mods, 0, 0, B, tm=512).reshape(B * NC, D)
    scale = 1.0 / math.sqrt(dh)
    gain = jnp.concatenate([jnp.tile(na_q_gain[0] * scale, H), jnp.tile(na_k_gain[0], H),
                            jnp.ones((D,), F32)])[None, :]
    w_qkv = na_w_qkv[0]
    qkv = qkv_proj(h, w_qkv, gain, 0, 3 * D, dh, 2 * D, tm=1024, tn=256)
    kvc = qkv_proj(hc, w_qkv, gain, D, 2 * D, dh, 2 * D, tm=1024, tn=256)
    rpb_pad = jnp.pad(na_rpb[0], ((0, 0), (0, 2 * NA_KH - (2 * NA_KH - 1)), (0, LANE - (2 * NA_KW - 1))))
    o = attention(qkv, kvc, rpb_pad, cfg)
    x1 = proj_residual(o, na_w_out[0][None], x.reshape(B * N, D), mods, 0, 2, N, None, tm=1024, tn=256)
    x2, ssq = moe_block(x1, mods, 0, gffn3, moe_w_router[0], moe_w_gate[0], moe_w_up[0], moe_w_down[0], cfg)

    pooled = pool_mixer_input(x2.reshape(B, N, D), ssq, gmix3, mods, 1, cfg, cw=256).reshape(B * N, D)
    x3 = proj_residual(pooled, pool_w[0], x2, mods, 1, 2, N, pool_scale[0][None, :], tm=1024, tn=256)
    x4, _ = moe_block(x3, mods, 1, gffn3, moe_w_router[1], moe_w_gate[1], moe_w_up[1], moe_w_down[1], cfg)
    return x4.reshape(B, N, D)


def kernel(x, c, ctx, c_ctx, ada_w, ada_b, norm_mix_g, norm_ffn_g, na_w_qkv, na_q_gain, na_k_gain, na_rpb, na_w_out, pool_w, pool_scale, moe_w_router, moe_w_gate, moe_w_up, moe_w_down):
    B, N, D = x.shape
    E, _, F = moe_w_gate.shape[1:]
    G = pool_w.shape[1]
    cfg = Cfg(B=B, N=N, D=D, NC=ctx.shape[1], H=na_rpb.shape[1], GW=64, E=E,
              CAP=max(1, 2 * N // E), F=F, G=G, windows=(2, 4, 8, 16), n_ada=6)
    return forward(cfg, x, c, ctx, c_ctx, ada_w, ada_b, norm_mix_g, norm_ffn_g, na_w_qkv, na_q_gain,
                   na_k_gain, na_rpb, na_w_out, pool_w, pool_scale, moe_w_router, moe_w_gate,
                   moe_w_up, moe_w_down)
```

```python
import functools
import math
from typing import NamedTuple

import jax
import jax.numpy as jnp
from jax import lax
from jax.experimental import pallas as pl
from jax.experimental.pallas import tpu as pltpu

F32 = jnp.float32
BF16 = jnp.bfloat16
I32 = jnp.int32

LANE = 128
NORM_EPS = 1e-6
NEG_INF = -1e30
NA_KH = 8
NA_KW = 16
ATTN_QROWS = 4
ATTN_KROWS = ATTN_QROWS + NA_KH
TOK_TILE = 256
COMB_WIN = 64
BF16_ROWS = 16
MM_CHUNK = 512
VMEM_LIMIT = 56 * 1024 * 1024


class Cfg(NamedTuple):
    B: int
    N: int
    D: int
    NC: int
    H: int
    GW: int
    E: int
    CAP: int
    F: int
    G: int
    windows: tuple
    n_ada: int


def _cp(sem, vmem=VMEM_LIMIT):
    return pltpu.CompilerParams(dimension_semantics=sem, vmem_limit_bytes=vmem)


def _dot(a, b):
    return jnp.dot(a, b, preferred_element_type=F32)


def _dot_nt(a, b):
    return lax.dot_general(a, b, (((1,), (1,)), ((), ())), preferred_element_type=F32)


def _silu(x):
    return x * (1.0 / (1.0 + jnp.exp(-x)))


def _row_chunks(tm):
    mc = min(MM_CHUNK, tm)
    return [(r, mc) for r in range(0, tm, mc)]


def _ada_kernel(cond_ref, w_ref, b_ref, o_ref):
    s = _silu(cond_ref[...]).astype(BF16)
    o_ref[0] = _dot(s, w_ref[0].astype(BF16)) + b_ref[0]


def ada_mods(cond8, ada_w, ada_b):
    L, D, ND = ada_w.shape
    tn = min(512, ND)
    return pl.pallas_call(
        _ada_kernel,
        grid=(L, ND // tn),
        in_specs=[pl.BlockSpec((8, D), lambda l, j: (0, 0)),
                  pl.BlockSpec((1, D, tn), lambda l, j: (l, 0, j)),
                  pl.BlockSpec((1, 1, tn), lambda l, j: (l, 0, j))],
        out_specs=pl.BlockSpec((1, 8, tn), lambda l, j: (l, 0, j)),
        out_shape=jax.ShapeDtypeStruct((L, 8, ND), F32),
        compiler_params=_cp(("parallel", "parallel")),
        name="ada",
    )(cond8, ada_w, ada_b.reshape(L, 1, ND))


def _norm_mod(x, g, sh, sc):
    ms = jnp.mean(x * x, axis=-1, keepdims=True)
    y = x * lax.rsqrt(ms + NORM_EPS) * g
    return y * (1.0 + sc) + sh


def _normmod_kernel(x_ref, g_ref, sh_ref, sc_ref, o_ref, *, row):
    r = pl.program_id(0) if row is None else row
    sh = sh_ref[0, pl.ds(r, 1), :]
    sc = sc_ref[0, pl.ds(r, 1), :]
    o_ref[0] = _norm_mod(x_ref[0], g_ref[0], sh, sc).astype(o_ref.dtype)


def _mod_spec(D, layer, chunk):
    return pl.BlockSpec((1, 8, D), lambda b, i: (layer, 0, chunk))


def normmod(x3, g3, mods, layer, sh_chunk, row, tm):
    B, n, D = x3.shape
    tm = min(tm, n)
    return pl.pallas_call(
        functools.partial(_normmod_kernel, row=row),
        grid=(B, n // tm),
        in_specs=[pl.BlockSpec((1, tm, D), lambda b, i: (b, i, 0)),
                  pl.BlockSpec((1, 1, D), lambda b, i: (layer, 0, 0)),
                  _mod_spec(D, layer, sh_chunk),
                  _mod_spec(D, layer, sh_chunk + 1)],
        out_specs=pl.BlockSpec((1, tm, D), lambda b, i: (b, i, 0)),
        out_shape=jax.ShapeDtypeStruct((B, n, D), BF16),
        compiler_params=_cp(("parallel", "parallel")),
        name="normmod",
    )(x3, g3, mods, mods)


def _qkv_kernel(a_ref, w_ref, gain_ref, o_ref, *, norm, dh):
    wb = w_ref[...].astype(BF16)
    tn = wb.shape[1]
    for r0, mc in _row_chunks(a_ref.shape[0]):
        acc = _dot(a_ref[r0:r0 + mc, :], wb)
        if norm:
            for c in range(tn // dh):
                blk = acc[:, c * dh:(c + 1) * dh]
                ms = jnp.mean(blk * blk, axis=-1, keepdims=True)
                y = blk * lax.rsqrt(ms + NORM_EPS) * gain_ref[:, c * dh:(c + 1) * dh]
                o_ref[r0:r0 + mc, c * dh:(c + 1) * dh] = y.astype(o_ref.dtype)
        else:
            o_ref[r0:r0 + mc, :] = acc.astype(o_ref.dtype)


def qkv_proj(a, w, gain, col0, ncols, dh, norm, tm, tn):
    M, D = a.shape
    tm = min(tm, M)
    off = col0 // tn
    return pl.pallas_call(
        functools.partial(_qkv_kernel, norm=norm, dh=dh),
        grid=(M // tm, ncols // tn),
        in_specs=[pl.BlockSpec((tm, D), lambda i, j: (i, 0)),
                  pl.BlockSpec((D, tn), lambda i, j: (0, j + off)),
                  pl.BlockSpec((1, tn), lambda i, j: (0, j + off))],
        out_specs=pl.BlockSpec((tm, tn), lambda i, j: (i, j)),
        out_shape=jax.ShapeDtypeStruct((M, ncols), BF16),
        compiler_params=_cp(("parallel", "arbitrary")),
        name="qkv",
    )(a, w, gain)


def _attn_bias_variants():
    out = []
    for var in range(3):
        tab = {}
        for i in range(ATTN_QROWS):
            for j in range(ATTN_KROWS):
                if var == 0:
                    valid, dr = j < NA_KH, j - i + NA_KH - 1
                elif var == 1:
                    valid, dr = i <= j < i + NA_KH, j - i + NA_KH // 2 - 1
                else:
                    valid, dr = j >= ATTN_KROWS - NA_KH, j - i - 1
                tab[(i, j)] = (valid, dr)
        out.append(tab)
    return out


def _attn_kernel(rpb_ref, q_ref, k_ref, v_ref, kc_ref, vc_ref, o_ref, bias_ref, *, gw, rows):
    nblk = rows // ATTN_QROWS
    qn = ATTN_QROWS * gw
    kn = ATTN_KROWS * gw

    @pl.when(pl.program_id(1) == 0)
    def _build_bias():
        qc = lax.broadcasted_iota(I32, (gw, gw), 0)
        kc = lax.broadcasted_iota(I32, (gw, gw), 1)
        cs = jnp.clip(qc - NA_KW // 2, 0, gw - NA_KW)
        col_mask = jnp.where((kc >= cs) & (kc < cs + NA_KW), 0.0, NEG_INF).astype(F32)
        tiles = []
        for dr in range(2 * NA_KH - 1):
            r = jnp.broadcast_to(rpb_ref[0, dr:dr + 1, :], (gw, LANE))
            t = pltpu.roll(r, LANE - (NA_KW - 1), 1, stride=1, stride_axis=0)
            tiles.append(t[:, :gw] + col_mask)
        neg = jnp.full((gw, gw), NEG_INF, F32)
        for var, tab in enumerate(_attn_bias_variants()):
            for (i, j), (valid, dr) in tab.items():
                bias_ref[var, i * gw:(i + 1) * gw, j * gw:(j + 1) * gw] = tiles[dr] if valid else neg

    kc_all = kc_ref[...]
    vc_all = vc_ref[...]

    def body(blk, carry):
        r0 = blk * ATTN_QROWS
        start = jnp.clip(r0 - NA_KH // 2, 0, rows - ATTN_KROWS)
        var = jnp.where(blk == 0, 0, jnp.where(blk == nblk - 1, 2, 1))
        q0 = pl.multiple_of(r0 * gw, qn)
        k0 = pl.multiple_of(start * gw, gw)
        q = q_ref[pl.ds(q0, qn), :]
        s_lat = _dot_nt(q, k_ref[pl.ds(k0, kn), :]) + bias_ref[var]
        s_ctx = _dot_nt(q, kc_all)
        m = jnp.maximum(jnp.max(s_lat, axis=-1, keepdims=True), jnp.max(s_ctx, axis=-1, keepdims=True))
        p_lat = jnp.exp(s_lat - m)
        p_ctx = jnp.exp(s_ctx - m)
        den = jnp.sum(p_lat, axis=-1, keepdims=True) + jnp.sum(p_ctx, axis=-1, keepdims=True)
        o = _dot(p_lat.astype(BF16), v_ref[pl.ds(k0, kn), :]) + _dot(p_ctx.astype(BF16), vc_all)
        o_ref[pl.ds(q0, qn), :] = (o * (1.0 / den)).astype(o_ref.dtype)
        return carry

    lax.fori_loop(0, nblk, body, 0, unroll=2)


def attention(qk, v, kc, vc, rpb_pad, cfg):
    B, N, D, NC, H, GW = cfg.B, cfg.N, cfg.D, cfg.NC, cfg.H, cfg.GW
    dh = D // H
    rows = N // GW
    assert dh == LANE and rows >= ATTN_KROWS and rows % (2 * ATTN_QROWS) == 0 and GW >= NA_KW
    return pl.pallas_call(
        functools.partial(_attn_kernel, gw=GW, rows=rows),
        grid=(H, B),
        in_specs=[pl.BlockSpec((1, 2 * NA_KH, LANE), lambda h, b: (h, 0, 0)),
                  pl.BlockSpec((N, dh), lambda h, b: (b, h)),
                  pl.BlockSpec((N, dh), lambda h, b: (b, H + h)),
                  pl.BlockSpec((N, dh), lambda h, b: (b, h)),
                  pl.BlockSpec((NC, dh), lambda h, b: (b, h)),
                  pl.BlockSpec((NC, dh), lambda h, b: (b, h))],
        out_specs=pl.BlockSpec((N, dh), lambda h, b: (b, h)),
        out_shape=jax.ShapeDtypeStruct((B * N, D), BF16),
        scratch_shapes=[pltpu.VMEM((3, ATTN_QROWS * GW, ATTN_KROWS * GW), F32)],
        compiler_params=_cp(("arbitrary", "arbitrary")),
        name="attn",
    )(rpb_pad, qk, qk, v, kc, vc)


def _proj_res_kernel(a_ref, w_ref, x_ref, g_ref, *rest, tiles_per_batch, scaled):
    if scaled:
        ps_ref, o_ref = rest
    else:
        (o_ref,) = rest
    wb = w_ref[0].astype(BF16)
    b = pl.program_id(0) // tiles_per_batch
    g = g_ref[0, pl.ds(b, 1), :]
    for r0, mc in _row_chunks(a_ref.shape[0]):
        y = _dot(a_ref[r0:r0 + mc, :], wb)
        if scaled:
            y = y * ps_ref[...]
        o_ref[r0:r0 + mc, :] = x_ref[r0:r0 + mc, :] + g * y


def proj_residual(a, w3, x2, mods, layer, gate_chunk, n_per_batch, pscale, tm, tn):
    M = a.shape[0]
    G, K, KO = w3.shape
    Dout = G * KO
    tm = min(tm, n_per_batch)
    tn = min(tn, KO)
    nj = KO // tn
    scaled = pscale is not None
    in_specs = [pl.BlockSpec((tm, K), lambda i, g, j: (i, g)),
                pl.BlockSpec((1, K, tn), lambda i, g, j: (g, 0, j)),
                pl.BlockSpec((tm, tn), lambda i, g, j: (i, g * nj + j)),
                pl.BlockSpec((1, 8, tn), lambda i, g, j: (layer, 0, gate_chunk * (Dout // tn) + g * nj + j))]
    args = [a, w3, x2, mods]
    if scaled:
        in_specs.append(pl.BlockSpec((1, tn), lambda i, g, j: (0, g * nj + j)))
        args.append(pscale)
    return pl.pallas_call(
        functools.partial(_proj_res_kernel, tiles_per_batch=n_per_batch // tm, scaled=scaled),
        grid=(M // tm, G, nj),
        in_specs=in_specs,
        out_specs=pl.BlockSpec((tm, tn), lambda i, g, j: (i, g * nj + j)),
        out_shape=jax.ShapeDtypeStruct((M, Dout), F32),
        compiler_params=_cp(("parallel", "arbitrary", "arbitrary")),
        name="proj_res",
    )(*args)


def _split_bf16(x):
    hi = x.astype(BF16)
    lo = (x - hi.astype(F32)).astype(BF16)
    return hi, lo


def _ffn_norm_kernel(x_ref, g_ref, sh_ref, sc_ref, wr_ref, hfx_ref, *, E):
    b = pl.program_id(0)
    D = x_ref.shape[2]
    hf = _norm_mod(x_ref[0], g_ref[0], sh_ref[0, pl.ds(b, 1), :], sc_ref[0, pl.ds(b, 1), :])
    hfx_ref[0, :, 0:D] = hf
    h_hi, h_lo = _split_bf16(hf)
    w_hi, w_lo = _split_bf16(wr_ref[...])
    lg = _dot(h_hi, w_hi) + (_dot(h_hi, w_lo) + _dot(h_lo, w_hi))
    lane = lax.broadcasted_iota(I32, (1, LANE), 1)
    lg = jnp.where(lane < E, lg, NEG_INF)
    ex = jnp.exp(lg - jnp.max(lg, axis=-1, keepdims=True))
    hfx_ref[0, :, D:D + LANE] = ex / jnp.sum(ex, axis=-1, keepdims=True)


def ffn_norm_router(x3, g3, mods, layer, wr_pad, E, tm):
    B, N, D = x3.shape
    tm = min(tm, N)
    return pl.pallas_call(
        functools.partial(_ffn_norm_kernel, E=E),
        grid=(B, N // tm),
        in_specs=[pl.BlockSpec((1, tm, D), lambda b, i: (b, i, 0)),
                  pl.BlockSpec((1, 1, D), lambda b, i: (layer, 0, 0)),
                  _mod_spec(D, layer, 3),
                  _mod_spec(D, layer, 4),
                  pl.BlockSpec((D, LANE), lambda b, i: (0, 0))],
        out_specs=pl.BlockSpec((1, tm, D + LANE), lambda b, i: (b, i, 0)),
        out_shape=jax.ShapeDtypeStruct((B, N, D + LANE), F32),
        compiler_params=_cp(("parallel", "parallel")),
        name="ffn_norm",
    )(x3, g3, mods, mods, wr_pad)


def _route_kernel(aff_ref, idx_ref, posq_ref, starts_ref, t_ref, cum_ref, *, E, cap, N):
    b = pl.program_id(0)
    nblk = N // TOK_TILE
    logn = int(math.log2(N))
    for k in range(N // LANE):
        t_ref[:, k * LANE:(k + 1) * LANE] = aff_ref[0, k * LANE:(k + 1) * LANE, :].T
    bits = pltpu.bitcast(t_ref[0:E, :], I32)
    tok = lax.broadcasted_iota(I32, (1, N), 1)

    def count(pred):
        return jnp.sum(jnp.where(pred, 1, 0), axis=1, keepdims=True)

    def thr_step(i, thr):
        cand = thr | (jnp.int32(1) << (30 - i))
        return jnp.where(count(bits >= cand) >= cap, cand, thr)

    thr = lax.fori_loop(0, 31, thr_step, jnp.zeros((E, 1), I32))
    gt = bits > thr
    eq = bits == thr
    need = cap - count(gt)

    def tie_step(i, ans):
        cand = ans | (jnp.int32(1) << (logn - 1 - i))
        return jnp.where(count(eq & (tok < cand)) < need, cand, ans)

    last = lax.fori_loop(0, logn, tie_step, jnp.zeros((E, 1), I32))
    sel = gt | (eq & (tok <= last))
    sel_f = jnp.where(sel, 1.0, 0.0).astype(F32)

    ri = lax.broadcasted_iota(I32, (TOK_TILE, TOK_TILE), 0)
    ci = lax.broadcasted_iota(I32, (TOK_TILE, TOK_TILE), 1)
    tri = jnp.where(ri <= ci, 1.0, 0.0).astype(BF16)
    lane_s = lax.broadcasted_iota(I32, (1, LANE), 1)
    carry = jnp.zeros((E, 1), F32)
    starts = jnp.zeros((E, LANE), F32)
    for k in range(nblk):
        m = sel_f[:, k * TOK_TILE:(k + 1) * TOK_TILE]
        starts = jnp.where(lane_s == k, carry, starts)
        cum_ref[:, k * TOK_TILE:(k + 1) * TOK_TILE] = _dot(m.astype(BF16), tri) + carry
        carry = carry + jnp.sum(m, axis=1, keepdims=True)
    starts_ref[0] = jnp.where(lane_s == nblk, carry, starts).astype(I32)

    lane_c = lax.broadcasted_iota(I32, (1, cap), 1)

    def slot_step(c, out):
        cnt = jnp.sum(jnp.where(cum_ref[...] <= lax.convert_element_type(c, F32), 1, 0), axis=1, keepdims=True)
        return jnp.where(lane_c == c, cnt, out)

    idx = lax.fori_loop(0, cap, slot_step, jnp.zeros((E, cap), I32), unroll=4)
    idx_ref[0] = idx + b * N

    t_ref[0:E, :] = jnp.where(sel, cum_ref[...], 0.0)
    t_ref[E:LANE, :] = jnp.zeros((LANE - E, N), F32)
    for k in range(N // LANE):
        posq_ref[0, k * LANE:(k + 1) * LANE, :] = t_ref[:, k * LANE:(k + 1) * LANE].T


def route(hfx, cfg):
    B, N, D, E, cap = cfg.B, cfg.N, cfg.D, cfg.E, cfg.CAP
    nblk = N // TOK_TILE
    assert N % TOK_TILE == 0 and cap % LANE == 0 and (1 << int(math.log2(N))) == N
    assert E % 8 == 0 and nblk < LANE
    return pl.pallas_call(
        functools.partial(_route_kernel, E=E, cap=cap, N=N),
        grid=(B,),
        in_specs=[pl.BlockSpec((1, N, LANE), lambda b: (b, 0, D // LANE))],
        out_specs=[pl.BlockSpec((1, E, cap), lambda b: (b, 0, 0)),
                   pl.BlockSpec((1, N, LANE), lambda b: (b, 0, 0)),
                   pl.BlockSpec((1, E, LANE), lambda b: (b, 0, 0))],
        out_shape=[jax.ShapeDtypeStruct((B, E, cap), I32),
                   jax.ShapeDtypeStruct((B, N, LANE), F32),
                   jax.ShapeDtypeStruct((B, E, LANE), I32)],
        scratch_shapes=[pltpu.VMEM((LANE, N), F32), pltpu.VMEM((E, N), F32)],
        compiler_params=_cp(("arbitrary",)),
        name="route",
    )(hfx)


def _gather_kernel(idx_ref, hfx_hbm, xs_ref, ga_ref, buf, sem, *, R, D):
    i = pl.program_id(0)
    n = pl.num_programs(0)

    def issue(step, slot):
        def body(r, carry):
            tok = idx_ref[step * R + r]
            pltpu.make_async_copy(hfx_hbm.at[pl.ds(tok, 1), :], buf.at[slot, pl.ds(r, 1), :],
                                  sem.at[slot]).start()
            return carry
        lax.fori_loop(0, R, body, 0, unroll=8)

    @pl.when(i == 0)
    def _():
        issue(0, 0)

    @pl.when(i + 1 < n)
    def _():
        issue(i + 1, (i + 1) % 2)

    slot = i % 2
    pltpu.make_async_copy(hfx_hbm.at[pl.ds(0, R), :], buf.at[slot], sem.at[slot]).wait()
    xs_ref[...] = buf[slot, :, 0:D].astype(xs_ref.dtype)
    ga_ref[...] = buf[slot, :, D:D + LANE]


def gather_rows(idx_flat, hfx2, R):
    M = idx_flat.shape[0]
    DX = hfx2.shape[1]
    D = DX - LANE
    return pl.pallas_call(
        functools.partial(_gather_kernel, R=R, D=D),
        grid_spec=pltpu.PrefetchScalarGridSpec(
            num_scalar_prefetch=1,
            grid=(M // R,),
            in_specs=[pl.BlockSpec(memory_space=pl.ANY)],
            out_specs=[pl.BlockSpec((R, D), lambda i, idx: (i, 0)),
                       pl.BlockSpec((R, LANE), lambda i, idx: (i, 0))],
            scratch_shapes=[pltpu.VMEM((2, R, DX), F32), pltpu.SemaphoreType.DMA((2,))]),
        out_shape=[jax.ShapeDtypeStruct((M, D), BF16), jax.ShapeDtypeStruct((M, LANE), F32)],
        compiler_params=_cp(("arbitrary",)),
        name="gather",
    )(idx_flat, hfx2)


def _expert_kernel(xs_ref, wg_ref, wu_ref, wd_ref, ga_ref, o_ref, hid_ref, *, nf, tf):
    e = pl.program_id(0)
    s = pl.program_id(1)
    rows = xs_ref.shape[0]

    @pl.when(s < nf)
    def _():
        wg = wg_ref[0].astype(BF16)
        wu = wu_ref[0].astype(BF16)
        for r0, mc in _row_chunks(rows):
            x = xs_ref[r0:r0 + mc, :]
            hid_ref[s, r0:r0 + mc, :] = (_silu(_dot(x, wg)) * _dot(x, wu)).astype(BF16)

    @pl.when(s >= nf)
    def _():
        wd = wd_ref[0].astype(BF16)
        lane = lax.broadcasted_iota(I32, (1, LANE), 1)
        for r0, mc in _row_chunks(rows):
            acc = _dot(hid_ref[0, r0:r0 + mc, :], wd[0:tf, :])
            for k in range(1, nf):
                acc = acc + _dot(hid_ref[k, r0:r0 + mc, :], wd[k * tf:(k + 1) * tf, :])
            gate = jnp.sum(jnp.where(lane == e, ga_ref[r0:r0 + mc, :], 0.0), axis=-1, keepdims=True)
            o_ref[r0:r0 + mc, :] = (acc * gate).astype(o_ref.dtype)


def experts(xs, gaff, w_gate, w_up, w_down, layer, tf, tn):
    _, E, D, F = w_gate.shape
    M = xs.shape[0]
    rows = M // E
    tf = min(tf, F)
    tn = min(tn, D)
    nf, nn = F // tf, D // tn
    return pl.pallas_call(
        functools.partial(_expert_kernel, nf=nf, tf=tf),
        grid=(E, nf + nn),
        in_specs=[pl.BlockSpec((rows, D), lambda e, s: (e, 0)),
                  pl.BlockSpec((None, 1, D, tf), lambda e, s: (layer, e, 0, jnp.minimum(s, nf - 1))),
                  pl.BlockSpec((None, 1, D, tf), lambda e, s: (layer, e, 0, jnp.minimum(s, nf - 1))),
                  pl.BlockSpec((None, 1, F, tn), lambda e, s: (layer, e, 0, jnp.maximum(s - nf, 0))),
                  pl.BlockSpec((rows, LANE), lambda e, s: (e, 0))],
        out_specs=pl.BlockSpec((rows, tn), lambda e, s: (e, jnp.maximum(s - nf, 0))),
        out_shape=jax.ShapeDtypeStruct((M, D), BF16),
        scratch_shapes=[pltpu.VMEM((nf, rows, tf), BF16)],
        compiler_params=_cp(("arbitrary", "arbitrary")),
        name="experts",
    )(xs, w_gate, w_up, w_down, gaff)


def _combine_kernel(st_ref, x_ref, g_ref, posq_ref, yg_hbm, o_ref, ssq_ref, ybuf, acc_ref, sem,
                    *, E, cap, B, nt):
    b = pl.program_id(0)
    t = pl.program_id(1)
    W = COMB_WIN
    KW = E * W
    step = b * nt + t
    slot = step % 2

    def tile_lo(bb, tt):
        base = (bb * (nt + 1) + tt) * E
        return [(st_ref[base + e] // BF16_ROWS) * BF16_ROWS for e in range(E)]

    def window_copy(bb, e, start, sl):
        row0 = pl.multiple_of((e * B + bb) * cap + start, BF16_ROWS)
        return pltpu.make_async_copy(yg_hbm.at[pl.ds(row0, W), :], ybuf.at[sl, pl.ds(e * W, W), :], sem.at[sl])

    def start_round0(bb, tt, sl):
        lo_n = tile_lo(bb, tt)
        for e in range(E):
            window_copy(bb, e, jnp.minimum(lo_n[e], cap - W), sl).start()

    @pl.when(step == 0)
    def _():
        start_round0(b, t, slot)

    @pl.when(step + 1 < B * nt)
    def _():
        start_round0((step + 1) // nt, (step + 1) % nt, 1 - slot)

    lo = tile_lo(b, t)
    base = (b * (nt + 1) + t) * E
    hi = [st_ref[base + E + e] for e in range(E)]
    nrounds = jnp.int32(1)
    for e in range(E):
        nrounds = jnp.maximum(nrounds, (hi[e] - lo[e] + (W - 1)) // W)

    qi = posq_ref[0].astype(I32)
    erow = lax.broadcasted_iota(I32, (LANE, KW), 0)
    ecol = lax.broadcasted_iota(I32, (LANE, KW), 1) // W
    expand = jnp.where(erow == ecol, 1.0, 0.0).astype(BF16)
    qe = (_dot((qi >> 5).astype(F32).astype(BF16), expand) * 32.0
          + _dot((qi & 31).astype(F32).astype(BF16), expand)).astype(I32) - 1
    lane_e = lax.broadcasted_iota(I32, (1, KW), 1) // W
    lane_k = lax.broadcasted_iota(I32, (1, KW), 1) % W

    def one_round(r, prefetched):
        lo_r = [lo[e] + r * W for e in range(E)]
        st = [jnp.minimum(lo_r[e], cap - W) for e in range(E)]
        if not prefetched:
            for e in range(E):
                window_copy(b, e, st[e], slot).start()
        stv = jnp.zeros((1, KW), I32)
        lov = jnp.zeros((1, KW), I32)
        for e in range(E):
            stv = jnp.where(lane_e == e, st[e], stv)
            lov = jnp.where(lane_e == e, lo_r[e], lov)
        onehot = (qe - stv == lane_k) & (qe >= lov) & (qe < lov + W)
        s = jnp.where(onehot, 1.0, 0.0).astype(BF16)
        for e in range(E):
            window_copy(b, e, st[e], slot).wait()
        return _dot(s, ybuf[slot])

    acc_ref[...] = one_round(0, True)

    def extra(r, carry):
        acc_ref[...] += one_round(r, False)
        return carry

    lax.fori_loop(1, nrounds, extra, 0)
    x2 = x_ref[...] + g_ref[0, pl.ds(b, 1), :] * acc_ref[...]
    o_ref[...] = x2
    ssq_ref[...] = jnp.sum(x2 * x2, axis=-1, keepdims=True)


def combine(starts_flat, x2d, mods, layer, posq, yg, cfg):
    B, N, D, E, cap = cfg.B, cfg.N, cfg.D, cfg.E, cfg.CAP
    nt = N // TOK_TILE
    assert cap >= COMB_WIN and cap % COMB_WIN == 0 and COMB_WIN % BF16_ROWS == 0 and cap <= 32 * 32
    return pl.pallas_call(
        functools.partial(_combine_kernel, E=E, cap=cap, B=B, nt=nt),
        grid_spec=pltpu.PrefetchScalarGridSpec(
            num_scalar_prefetch=1,
            grid=(B, nt),
            in_specs=[pl.BlockSpec((TOK_TILE, D), lambda b, t, st: (b * nt + t, 0)),
                      pl.BlockSpec((1, 8, D), lambda b, t, st: (layer, 0, 5)),
                      pl.BlockSpec((1, TOK_TILE, LANE), lambda b, t, st: (b, t, 0)),
                      pl.BlockSpec(memory_space=pl.ANY)],
            out_specs=[pl.BlockSpec((TOK_TILE, D), lambda b, t, st: (b * nt + t, 0)),
                       pl.BlockSpec((TOK_TILE, 1), lambda b, t, st: (b * nt + t, 0))],
            scratch_shapes=[pltpu.VMEM((2, E * COMB_WIN, D), BF16),
                            pltpu.VMEM((TOK_TILE, D), F32),
                            pltpu.SemaphoreType.DMA((2,))]),
        out_shape=[jax.ShapeDtypeStruct((B * N, D), F32),
                   jax.ShapeDtypeStruct((B * N, 1), F32)],
        compiler_params=_cp(("arbitrary", "arbitrary")),
        name="combine",
    )(starts_flat, x2d, mods, posq, yg)


def _pool_kernel(x_ref, ssq_ref, g_ref, sh_ref, sc_ref, o_ref, hs_ref, *, N, D, windows, chunks_per_group):
    b = pl.program_id(0)
    j = pl.program_id(1)
    P = max(windows) // 2
    cw = x_ref.shape[2]
    nblk = N // TOK_TILE
    g = g_ref[0]
    sh = sh_ref[0, pl.ds(b, 1), :]
    sc = sc_ref[0, pl.ds(b, 1), :]
    zeros = jnp.zeros((P, cw), F32)
    hs_ref[0:P, :] = zeros
    hs_ref[P + N:P + N + P, :] = zeros

    def fill(k, carry):
        r0 = pl.multiple_of(k * TOK_TILE, TOK_TILE)
        rstd = lax.rsqrt(ssq_ref[pl.ds(r0, TOK_TILE), :] * (1.0 / D) + NORM_EPS)
        y = x_ref[0, pl.ds(r0, TOK_TILE), :] * rstd * g
        hs_ref[pl.ds(pl.multiple_of(P + r0, 8), TOK_TILE), :] = y * (1.0 + sc) + sh
        return carry

    lax.fori_loop(0, nblk, fill, 0)

    for grp, w in enumerate(windows):
        half = w // 2

        @pl.when(j // chunks_per_group == grp)
        def _(half=half):
            def pool(k, carry):
                r0 = pl.multiple_of(k * TOK_TILE, TOK_TILE)
                big = hs_ref[pl.ds(r0, TOK_TILE + 2 * P), :]
                acc = big[P - half:P - half + TOK_TILE]
                for d in range(-half + 1, half):
                    acc = acc + big[P + d:P + d + TOK_TILE]
                tk = lax.broadcasted_iota(I32, (TOK_TILE, 1), 0) + r0
                cnt = (jnp.minimum(tk + half, N) - jnp.maximum(tk - half, 0)).astype(F32)
                o_ref[0, pl.ds(r0, TOK_TILE), :] = (acc / cnt - big[P:P + TOK_TILE]).astype(o_ref.dtype)
                return carry

            lax.fori_loop(0, nblk, pool, 0)


def pool_mixer_input(x3, ssq, g3, mods, layer, cfg, cw):
    B, N, D = x3.shape
    dg = D // cfg.G
    cw = min(cw, dg)
    P = max(cfg.windows) // 2
    nj = D // cw

    def mspec(chunk):
        return pl.BlockSpec((1, 8, cw), lambda b, j: (layer, 0, chunk * nj + j))

    return pl.pallas_call(
        functools.partial(_pool_kernel, N=N, D=D, windows=cfg.windows, chunks_per_group=dg // cw),
        grid=(B, nj),
        in_specs=[pl.BlockSpec((1, N, cw), lambda b, j: (b, 0, j)),
                  pl.BlockSpec((N, 1), lambda b, j: (b, 0)),
                  pl.BlockSpec((1, 1, cw), lambda b, j: (layer, 0, j)),
                  mspec(0), mspec(1)],
        out_specs=pl.BlockSpec((1, N, cw), lambda b, j: (b, 0, j)),
        out_shape=jax.ShapeDtypeStruct((B, N, D), BF16),
        scratch_shapes=[pltpu.VMEM((N + 2 * P, cw), F32)],
        compiler_params=_cp(("parallel", "parallel")),
        name="pool",
    )(x3, ssq, g3, mods, mods)


def moe_block(x2d, mods, layer, g_ffn3, w_router, w_gate, w_up, w_down, cfg):
    B, N, D, E, cap = cfg.B, cfg.N, cfg.D, cfg.E, cfg.CAP
    nt = N // TOK_TILE
    wr_pad = jnp.pad(w_router, ((0, 0), (0, LANE - E)))
    hfx = ffn_norm_router(x2d.reshape(B, N, D), g_ffn3, mods, layer, wr_pad, E, tm=512)
    idx, posq, starts = route(hfx, cfg)
    idx_flat = jnp.transpose(idx, (1, 0, 2)).reshape(-1)
    starts_flat = jnp.transpose(starts[:, :, :nt + 1], (0, 2, 1)).reshape(-1)
    xs, gaff = gather_rows(idx_flat, hfx.reshape(B * N, D + LANE), R=min(256, cap))
    yg = experts(xs, gaff, w_gate, w_up, w_down, layer, tf=256, tn=512)
    return combine(starts_flat, x2d, mods, layer, posq, yg, cfg)


def forward(cfg, x, c, ctx, c_ctx, ada_w, ada_b, norm_mix_g, norm_ffn_g, na_w_qkv, na_q_gain,
            na_k_gain, na_rpb, na_w_out, pool_w, pool_scale, moe_w_router, moe_w_gate,
            moe_w_up, moe_w_down):
    B, N, D, NC, H = cfg.B, cfg.N, cfg.D, cfg.NC, cfg.H
    dh = D // H
    L = ada_w.shape[0]
    assert L == 2 and B + 1 <= 8
    cond8 = jnp.concatenate([c, c_ctx[None, :], jnp.zeros((8 - B - 1, D), F32)], axis=0)
    mods = ada_mods(cond8, ada_w, ada_b)
    gmix3 = norm_mix_g.reshape(L, 1, D)
    gffn3 = norm_ffn_g.reshape(L, 1, D)

    h = normmod(x, gmix3, mods, 0, 0, None, tm=512).reshape(B * N, D)
    hc = normmod(ctx, gmix3, mods, 0, 0, B, tm=512).reshape(B * NC, D)
    scale = 1.0 / math.sqrt(dh)
    gain = jnp.concatenate([jnp.tile(na_q_gain[0] * scale, H), jnp.tile(na_k_gain[0], H),
                            jnp.ones((D,), F32)])[None, :]
    w_qkv = na_w_qkv[0]
    qk = qkv_proj(h, w_qkv, gain, 0, 2 * D, dh, True, tm=1024, tn=256)
    v = qkv_proj(h, w_qkv, gain, 2 * D, D, dh, False, tm=1024, tn=256)
    kc = qkv_proj(hc, w_qkv, gain, D, D, dh, True, tm=1024, tn=256)
    vc = qkv_proj(hc, w_qkv, gain, 2 * D, D, dh, False, tm=1024, tn=256)
    rpb_pad = jnp.pad(na_rpb[0], ((0, 0), (0, 1), (0, LANE - (2 * NA_KW - 1))))
    o = attention(qk, v, kc, vc, rpb_pad, cfg)
    x1 = proj_residual(o, na_w_out[0][None], x.reshape(B * N, D), mods, 0, 2, N, None, tm=1024, tn=256)
    x2, ssq = moe_block(x1, mods, 0, gffn3, moe_w_router[0], moe_w_gate, moe_w_up, moe_w_down, cfg)

    pooled = pool_mixer_input(x2.reshape(B, N, D), ssq, gmix3, mods, 1, cfg, cw=256).reshape(B * N, D)
    x3 = proj_residual(pooled, pool_w[0], x2, mods, 1, 2, N, pool_scale[0][None, :], tm=1024, tn=1024)
    x4, _ = moe_block(x3, mods, 1, gffn3, moe_w_router[1], moe_w_gate, moe_w_up, moe_w_down, cfg)
    return x4.reshape(B, N, D)


def kernel(x, c, ctx, c_ctx, ada_w, ada_b, norm_mix_g, norm_ffn_g, na_w_qkv, na_q_gain, na_k_gain, na_rpb, na_w_out, pool_w, pool_scale, moe_w_router, moe_w_gate, moe_w_up, moe_w_down):
    B, N, D = x.shape
    E, _, F = moe_w_gate.shape[1:]
    G = pool_w.shape[1]
    cfg = Cfg(B=B, N=N, D=D, NC=ctx.shape[1], H=na_rpb.shape[1], GW=64, E=E,
              CAP=max(1, 2 * N // E), F=F, G=G, windows=(2, 4, 8, 16), n_ada=6)
    return forward(cfg, x, c, ctx, c_ctx, ada_w, ada_b, norm_mix_g, norm_ffn_g, na_w_qkv, na_q_gain,
                   na_k_gain, na_rpb, na_w_out, pool_w, pool_scale, moe_w_router, moe_w_gate,
                   moe_w_up, moe_w_down)
```

```python
import functools
import math
from typing import NamedTuple

import jax
import jax.numpy as jnp
from jax import lax
from jax.experimental import pallas as pl
from jax.experimental.pallas import tpu as pltpu

F32 = jnp.float32
BF16 = jnp.bfloat16
I32 = jnp.int32

LANE = 128
NORM_EPS = 1e-6
NEG_INF = -1e30
NA_KH = 8
NA_KW = 16
ATTN_QROWS = 4
ATTN_KROWS = ATTN_QROWS + NA_KH
TOK_TILE = 256
COMB_WIN = 64
BF16_ROWS = 16
MM_CHUNK = 512
VMEM_LIMIT = 56 * 1024 * 1024


class Cfg(NamedTuple):
    B: int
    N: int
    D: int
    NC: int
    H: int
    GW: int
    E: int
    CAP: int
    F: int
    G: int
    windows: tuple
    n_ada: int


def _cp(sem, vmem=VMEM_LIMIT):
    return pltpu.CompilerParams(dimension_semantics=sem, vmem_limit_bytes=vmem)


def _dot(a, b):
    return jnp.dot(a, b, preferred_element_type=F32)


def _dot_nt(a, b):
    return lax.dot_general(a, b, (((1,), (1,)), ((), ())), preferred_element_type=F32)


def _silu(x):
    return x * (1.0 / (1.0 + jnp.exp(-x)))


def _row_chunks(tm):
    mc = min(MM_CHUNK, tm)
    return [(r, mc) for r in range(0, tm, mc)]


def _ada_kernel(cond_ref, w_ref, b_ref, o_ref):
    s = _silu(cond_ref[...]).astype(BF16)
    o_ref[0] = _dot(s, w_ref[0].astype(BF16)) + b_ref[0]


def ada_mods(cond8, ada_w, ada_b):
    L, D, ND = ada_w.shape
    tn = min(512, ND)
    return pl.pallas_call(
        _ada_kernel,
        grid=(L, ND // tn),
        in_specs=[pl.BlockSpec((8, D), lambda l, j: (0, 0)),
                  pl.BlockSpec((1, D, tn), lambda l, j: (l, 0, j)),
                  pl.BlockSpec((1, 1, tn), lambda l, j: (l, 0, j))],
        out_specs=pl.BlockSpec((1, 8, tn), lambda l, j: (l, 0, j)),
        out_shape=jax.ShapeDtypeStruct((L, 8, ND), F32),
        compiler_params=_cp(("parallel", "parallel")),
        name="ada",
    )(cond8, ada_w, ada_b.reshape(L, 1, ND))


def _norm_mod(x, g, sh, sc):
    ms = jnp.mean(x * x, axis=-1, keepdims=True)
    y = x * lax.rsqrt(ms + NORM_EPS) * g
    return y * (1.0 + sc) + sh


def _normmod_kernel(x_ref, g_ref, sh_ref, sc_ref, o_ref, *, row):
    r = pl.program_id(0) if row is None else row
    sh = sh_ref[0, pl.ds(r, 1), :]
    sc = sc_ref[0, pl.ds(r, 1), :]
    o_ref[0] = _norm_mod(x_ref[0], g_ref[0], sh, sc).astype(o_ref.dtype)


def _mod_spec(D, layer, chunk):
    return pl.BlockSpec((1, 8, D), lambda b, i: (layer, 0, chunk))


def normmod(x3, g3, mods, layer, sh_chunk, row, tm):
    B, n, D = x3.shape
    tm = min(tm, n)
    return pl.pallas_call(
        functools.partial(_normmod_kernel, row=row),
        grid=(B, n // tm),
        in_specs=[pl.BlockSpec((1, tm, D), lambda b, i: (b, i, 0)),
                  pl.BlockSpec((1, 1, D), lambda b, i: (layer, 0, 0)),
                  _mod_spec(D, layer, sh_chunk),
                  _mod_spec(D, layer, sh_chunk + 1)],
        out_specs=pl.BlockSpec((1, tm, D), lambda b, i: (b, i, 0)),
        out_shape=jax.ShapeDtypeStruct((B, n, D), BF16),
        compiler_params=_cp(("parallel", "parallel")),
        name="normmod",
    )(x3, g3, mods, mods)


def _qkv_kernel(a_ref, w_ref, gain_ref, o_ref, *, norm, dh):
    wb = w_ref[...].astype(BF16)
    tn = wb.shape[1]
    for r0, mc in _row_chunks(a_ref.shape[0]):
        acc = _dot(a_ref[r0:r0 + mc, :], wb)
        if norm:
            for c in range(tn // dh):
                blk = acc[:, c * dh:(c + 1) * dh]
                ms = jnp.mean(blk * blk, axis=-1, keepdims=True)
                y = blk * lax.rsqrt(ms + NORM_EPS) * gain_ref[:, c * dh:(c + 1) * dh]
                o_ref[r0:r0 + mc, c * dh:(c + 1) * dh] = y.astype(o_ref.dtype)
        else:
            o_ref[r0:r0 + mc, :] = acc.astype(o_ref.dtype)


def qkv_proj(a, w, gain, col0, ncols, dh, norm, tm, tn):
    M, D = a.shape
    tm = min(tm, M)
    off = col0 // tn
    return pl.pallas_call(
        functools.partial(_qkv_kernel, norm=norm, dh=dh),
        grid=(M // tm, ncols // tn),
        in_specs=[pl.BlockSpec((tm, D), lambda i, j: (i, 0)),
                  pl.BlockSpec((D, tn), lambda i, j: (0, j + off)),
                  pl.BlockSpec((1, tn), lambda i, j: (0, j + off))],
        out_specs=pl.BlockSpec((tm, tn), lambda i, j: (i, j)),
        out_shape=jax.ShapeDtypeStruct((M, ncols), BF16),
        compiler_params=_cp(("parallel", "arbitrary")),
        name="qkv",
    )(a, w, gain)


def _attn_bias_variants():
    out = []
    for var in range(3):
        tab = {}
        for i in range(ATTN_QROWS):
            for j in range(ATTN_KROWS):
                if var == 0:
                    valid, dr = j < NA_KH, j - i + NA_KH - 1
                elif var == 1:
                    valid, dr = i <= j < i + NA_KH, j - i + NA_KH // 2 - 1
                else:
                    valid, dr = j >= ATTN_KROWS - NA_KH, j - i - 1
                tab[(i, j)] = (valid, dr)
        out.append(tab)
    return out


def _attn_kernel(rpb_ref, q_ref, k_ref, v_ref, kc_ref, vc_ref, o_ref, bias_ref, sl_ref, sc_ref,
                 pl_ref, pc_ref, den_ref, *, gw, rows):
    nblk = rows // ATTN_QROWS
    qn = ATTN_QROWS * gw
    kn = ATTN_KROWS * gw

    @pl.when(pl.program_id(1) == 0)
    def _build_bias():
        qc = lax.broadcasted_iota(I32, (gw, gw), 0)
        kc = lax.broadcasted_iota(I32, (gw, gw), 1)
        cs = jnp.clip(qc - NA_KW // 2, 0, gw - NA_KW)
        col_mask = jnp.where((kc >= cs) & (kc < cs + NA_KW), 0.0, NEG_INF).astype(F32)
        tiles = []
        for dr in range(2 * NA_KH - 1):
            r = jnp.broadcast_to(rpb_ref[0, dr:dr + 1, :], (gw, LANE))
            t = pltpu.roll(r, LANE - (NA_KW - 1), 1, stride=1, stride_axis=0)
            tiles.append(t[:, :gw] + col_mask)
        neg = jnp.full((gw, gw), NEG_INF, F32)
        for var, tab in enumerate(_attn_bias_variants()):
            for (i, j), (valid, dr) in tab.items():
                bias_ref[var, i * gw:(i + 1) * gw, j * gw:(j + 1) * gw] = tiles[dr] if valid else neg

    def geom(blk):
        if isinstance(blk, int):
            r0 = blk * ATTN_QROWS
            start = min(max(r0 - NA_KH // 2, 0), rows - ATTN_KROWS)
            var = 0 if blk == 0 else (2 if blk == nblk - 1 else 1)
            return r0 * gw, start * gw, var
        r0 = blk * ATTN_QROWS
        start = jnp.clip(r0 - NA_KH // 2, 0, rows - ATTN_KROWS)
        var = jnp.where(blk == 0, 0, jnp.where(blk == nblk - 1, 2, 1))
        return pl.multiple_of(r0 * gw, qn), pl.multiple_of(start * gw, gw), var

    def scores(blk, slot):
        q0, k0, var = geom(blk)
        q = q_ref[pl.ds(q0, qn), :]
        sl_ref[slot] = _dot_nt(q, k_ref[pl.ds(k0, kn), :]) + bias_ref[var]
        sc_ref[slot] = _dot_nt(q, kc_ref[...])

    def softmax(slot):
        s_lat = sl_ref[slot]
        s_ctx = sc_ref[slot]
        m = jnp.maximum(jnp.max(s_lat, axis=-1, keepdims=True), jnp.max(s_ctx, axis=-1, keepdims=True))
        p_lat = jnp.exp(s_lat - m)
        p_ctx = jnp.exp(s_ctx - m)
        den_ref[slot] = jnp.sum(p_lat, axis=-1, keepdims=True) + jnp.sum(p_ctx, axis=-1, keepdims=True)
        pl_ref[slot] = p_lat.astype(BF16)
        pc_ref[slot] = p_ctx.astype(BF16)

    def values(blk, slot):
        q0, k0, _ = geom(blk)
        o = _dot(pl_ref[slot], v_ref[pl.ds(k0, kn), :]) + _dot(pc_ref[slot], vc_ref[...])
        o_ref[pl.ds(q0, qn), :] = (o * (1.0 / den_ref[slot])).astype(o_ref.dtype)

    scores(0, 0)
    scores(1, 1)
    softmax(0)

    def body(it, carry):
        i = 2 + 2 * it
        scores(i, 0)
        softmax(1)
        values(i - 2, 0)
        scores(i + 1, 1)
        softmax(0)
        values(i - 1, 1)
        return carry

    lax.fori_loop(0, (nblk - 2) // 2, body, 0)
    softmax(1)
    values(nblk - 2, 0)
    values(nblk - 1, 1)


def attention(qk, v, kc, vc, rpb_pad, cfg):
    B, N, D, NC, H, GW = cfg.B, cfg.N, cfg.D, cfg.NC, cfg.H, cfg.GW
    dh = D // H
    rows = N // GW
    assert dh == LANE and rows >= ATTN_KROWS and rows % (2 * ATTN_QROWS) == 0 and GW >= NA_KW
    return pl.pallas_call(
        functools.partial(_attn_kernel, gw=GW, rows=rows),
        grid=(H, B),
        in_specs=[pl.BlockSpec((1, 2 * NA_KH, LANE), lambda h, b: (h, 0, 0)),
                  pl.BlockSpec((N, dh), lambda h, b: (b, h)),
                  pl.BlockSpec((N, dh), lambda h, b: (b, H + h)),
                  pl.BlockSpec((N, dh), lambda h, b: (b, h)),
                  pl.BlockSpec((NC, dh), lambda h, b: (b, h)),
                  pl.BlockSpec((NC, dh), lambda h, b: (b, h))],
        out_specs=pl.BlockSpec((N, dh), lambda h, b: (b, h)),
        out_shape=jax.ShapeDtypeStruct((B * N, D), BF16),
        scratch_shapes=[pltpu.VMEM((3, ATTN_QROWS * GW, ATTN_KROWS * GW), F32),
                        pltpu.VMEM((2, ATTN_QROWS * GW, ATTN_KROWS * GW), F32),
                        pltpu.VMEM((2, ATTN_QROWS * GW, NC), F32),
                        pltpu.VMEM((2, ATTN_QROWS * GW, ATTN_KROWS * GW), BF16),
                        pltpu.VMEM((2, ATTN_QROWS * GW, NC), BF16),
                        pltpu.VMEM((2, ATTN_QROWS * GW, 1), F32)],
        compiler_params=_cp(("arbitrary", "arbitrary")),
        name="attn",
    )(rpb_pad, qk, qk, v, kc, vc)


def _proj_res_kernel(a_ref, w_ref, x_ref, g_ref, *rest, tiles_per_batch, scaled):
    if scaled:
        ps_ref, o_ref = rest
    else:
        (o_ref,) = rest
    wb = w_ref[0].astype(BF16)
    b = pl.program_id(0) // tiles_per_batch
    g = g_ref[0, pl.ds(b, 1), :]
    for r0, mc in _row_chunks(a_ref.shape[0]):
        y = _dot(a_ref[r0:r0 + mc, :], wb)
        if scaled:
            y = y * ps_ref[...]
        o_ref[r0:r0 + mc, :] = x_ref[r0:r0 + mc, :] + g * y


def proj_residual(a, w3, x2, mods, layer, gate_chunk, n_per_batch, pscale, tm, tn):
    M = a.shape[0]
    G, K, KO = w3.shape
    Dout = G * KO
    tm = min(tm, n_per_batch)
    tn = min(tn, KO)
    nj = KO // tn
    scaled = pscale is not None
    in_specs = [pl.BlockSpec((tm, K), lambda i, g, j: (i, g)),
                pl.BlockSpec((1, K, tn), lambda i, g, j: (g, 0, j)),
                pl.BlockSpec((tm, tn), lambda i, g, j: (i, g * nj + j)),
                pl.BlockSpec((1, 8, tn), lambda i, g, j: (layer, 0, gate_chunk * (Dout // tn) + g * nj + j))]
    args = [a, w3, x2, mods]
    if scaled:
        in_specs.append(pl.BlockSpec((1, tn), lambda i, g, j: (0, g * nj + j)))
        args.append(pscale)
    return pl.pallas_call(
        functools.partial(_proj_res_kernel, tiles_per_batch=n_per_batch // tm, scaled=scaled),
        grid=(M // tm, G, nj),
        in_specs=in_specs,
        out_specs=pl.BlockSpec((tm, tn), lambda i, g, j: (i, g * nj + j)),
        out_shape=jax.ShapeDtypeStruct((M, Dout), F32),
        compiler_params=_cp(("parallel", "arbitrary", "arbitrary")),
        name="proj_res",
    )(*args)


def _split_bf16(x):
    hi = x.astype(BF16)
    lo = (x - hi.astype(F32)).astype(BF16)
    return hi, lo


def _ffn_norm_kernel(x_ref, g_ref, sh_ref, sc_ref, wr_ref, hfx_ref, *, E):
    b = pl.program_id(0)
    D = x_ref.shape[2]
    hf = _norm_mod(x_ref[0], g_ref[0], sh_ref[0, pl.ds(b, 1), :], sc_ref[0, pl.ds(b, 1), :])
    hfx_ref[0, :, 0:D] = hf
    h_hi, h_lo = _split_bf16(hf)
    w_hi, w_lo = _split_bf16(wr_ref[...])
    lg = _dot(h_hi, w_hi) + (_dot(h_hi, w_lo) + _dot(h_lo, w_hi))
    lane = lax.broadcasted_iota(I32, (1, LANE), 1)
    lg = jnp.where(lane < E, lg, NEG_INF)
    ex = jnp.exp(lg - jnp.max(lg, axis=-1, keepdims=True))
    hfx_ref[0, :, D:D + LANE] = ex / jnp.sum(ex, axis=-1, keepdims=True)


def ffn_norm_router(x3, g3, mods, layer, wr_pad, E, tm):
    B, N, D = x3.shape
    tm = min(tm, N)
    return pl.pallas_call(
        functools.partial(_ffn_norm_kernel, E=E),
        grid=(B, N // tm),
        in_specs=[pl.BlockSpec((1, tm, D), lambda b, i: (b, i, 0)),
                  pl.BlockSpec((1, 1, D), lambda b, i: (layer, 0, 0)),
                  _mod_spec(D, layer, 3),
                  _mod_spec(D, layer, 4),
                  pl.BlockSpec((D, LANE), lambda b, i: (0, 0))],
        out_specs=pl.BlockSpec((1, tm, D + LANE), lambda b, i: (b, i, 0)),
        out_shape=jax.ShapeDtypeStruct((B, N, D + LANE), F32),
        compiler_params=_cp(("parallel", "parallel")),
        name="ffn_norm",
    )(x3, g3, mods, mods, wr_pad)


def _route_kernel(aff_ref, idx_ref, posq_ref, starts_ref, t_ref, cum_ref, *, E, cap, N):
    b = pl.program_id(0)
    nblk = N // TOK_TILE
    logn = int(math.log2(N))
    for k in range(N // LANE):
        t_ref[:, k * LANE:(k + 1) * LANE] = aff_ref[0, k * LANE:(k + 1) * LANE, :].T
    bits = pltpu.bitcast(t_ref[0:E, :], I32)
    tok = lax.broadcasted_iota(I32, (1, N), 1)

    def count(pred):
        return jnp.sum(jnp.where(pred, 1, 0), axis=1, keepdims=True)

    def thr_step(i, thr):
        cand = thr | (jnp.int32(1) << (30 - i))
        return jnp.where(count(bits >= cand) >= cap, cand, thr)

    thr = lax.fori_loop(0, 31, thr_step, jnp.zeros((E, 1), I32))
    gt = bits > thr
    eq = bits == thr
    need = cap - count(gt)

    def tie_step(i, ans):
        cand = ans | (jnp.int32(1) << (logn - 1 - i))
        return jnp.where(count(eq & (tok < cand)) < need, cand, ans)

    last = lax.fori_loop(0, logn, tie_step, jnp.zeros((E, 1), I32))
    sel = gt | (eq & (tok <= last))
    sel_f = jnp.where(sel, 1.0, 0.0).astype(F32)

    ri = lax.broadcasted_iota(I32, (TOK_TILE, TOK_TILE), 0)
    ci = lax.broadcasted_iota(I32, (TOK_TILE, TOK_TILE), 1)
    tri = jnp.where(ri <= ci, 1.0, 0.0).astype(BF16)
    lane_s = lax.broadcasted_iota(I32, (1, LANE), 1)
    carry = jnp.zeros((E, 1), F32)
    starts = jnp.zeros((E, LANE), F32)
    for k in range(nblk):
        m = sel_f[:, k * TOK_TILE:(k + 1) * TOK_TILE]
        starts = jnp.where(lane_s == k, carry, starts)
        cum_ref[:, k * TOK_TILE:(k + 1) * TOK_TILE] = _dot(m.astype(BF16), tri) + carry
        carry = carry + jnp.sum(m, axis=1, keepdims=True)
    starts_ref[0] = jnp.where(lane_s == nblk, carry, starts).astype(I32)

    lane_c = lax.broadcasted_iota(I32, (1, cap), 1)

    def slot_step(c, out):
        cnt = jnp.sum(jnp.where(cum_ref[...] <= lax.convert_element_type(c, F32), 1, 0), axis=1, keepdims=True)
        return jnp.where(lane_c == c, cnt, out)

    idx = lax.fori_loop(0, cap, slot_step, jnp.zeros((E, cap), I32), unroll=4)
    idx_ref[0] = idx + b * N

    t_ref[0:E, :] = jnp.where(sel, cum_ref[...], 0.0)
    t_ref[E:LANE, :] = jnp.zeros((LANE - E, N), F32)
    for k in range(N // LANE):
        posq_ref[0, k * LANE:(k + 1) * LANE, :] = t_ref[:, k * LANE:(k + 1) * LANE].T


def route(hfx, cfg):
    B, N, D, E, cap = cfg.B, cfg.N, cfg.D, cfg.E, cfg.CAP
    nblk = N // TOK_TILE
    assert N % TOK_TILE == 0 and cap % LANE == 0 and (1 << int(math.log2(N))) == N
    assert E % 8 == 0 and nblk < LANE
    return pl.pallas_call(
        functools.partial(_route_kernel, E=E, cap=cap, N=N),
        grid=(B,),
        in_specs=[pl.BlockSpec((1, N, LANE), lambda b: (b, 0, D // LANE))],
        out_specs=[pl.BlockSpec((1, E, cap), lambda b: (b, 0, 0)),
                   pl.BlockSpec((1, N, LANE), lambda b: (b, 0, 0)),
                   pl.BlockSpec((1, E, LANE), lambda b: (b, 0, 0))],
        out_shape=[jax.ShapeDtypeStruct((B, E, cap), I32),
                   jax.ShapeDtypeStruct((B, N, LANE), F32),
                   jax.ShapeDtypeStruct((B, E, LANE), I32)],
        scratch_shapes=[pltpu.VMEM((LANE, N), F32), pltpu.VMEM((E, N), F32)],
        compiler_params=_cp(("arbitrary",)),
        name="route",
    )(hfx)


def _gather_kernel(idx_ref, hfx_hbm, xs_ref, ga_ref, buf, sem, *, R, D):
    i = pl.program_id(0)
    n = pl.num_programs(0)

    def issue(step, slot):
        for r in range(R):
            tok = idx_ref[step * R + r]
            pltpu.make_async_copy(hfx_hbm.at[pl.ds(tok, 1), :], buf.at[slot, pl.ds(r, 1), :],
                                  sem.at[slot]).start()

    @pl.when(i == 0)
    def _():
        issue(0, 0)

    @pl.when(i + 1 < n)
    def _():
        issue(i + 1, (i + 1) % 2)

    slot = i % 2
    pltpu.make_async_copy(hfx_hbm.at[pl.ds(0, R), :], buf.at[slot], sem.at[slot]).wait()
    xs_ref[...] = buf[slot, :, 0:D].astype(xs_ref.dtype)
    ga_ref[...] = buf[slot, :, D:D + LANE]


def gather_rows(idx_flat, hfx2, R):
    M = idx_flat.shape[0]
    DX = hfx2.shape[1]
    D = DX - LANE
    return pl.pallas_call(
        functools.partial(_gather_kernel, R=R, D=D),
        grid_spec=pltpu.PrefetchScalarGridSpec(
            num_scalar_prefetch=1,
            grid=(M // R,),
            in_specs=[pl.BlockSpec(memory_space=pl.ANY)],
            out_specs=[pl.BlockSpec((R, D), lambda i, idx: (i, 0)),
                       pl.BlockSpec((R, LANE), lambda i, idx: (i, 0))],
            scratch_shapes=[pltpu.VMEM((2, R, DX), F32), pltpu.SemaphoreType.DMA((2,))]),
        out_shape=[jax.ShapeDtypeStruct((M, D), BF16), jax.ShapeDtypeStruct((M, LANE), F32)],
        compiler_params=_cp(("arbitrary",)),
        name="gather",
    )(idx_flat, hfx2)


def _expert_kernel(xs_ref, wg_ref, wu_ref, wd_ref, ga_ref, o_ref, hid_ref, *, nf, tf):
    e = pl.program_id(0)
    s = pl.program_id(1)
    rows = xs_ref.shape[0]

    @pl.when(s < nf)
    def _():
        wg = wg_ref[0].astype(BF16)
        wu = wu_ref[0].astype(BF16)
        for r0, mc in _row_chunks(rows):
            x = xs_ref[r0:r0 + mc, :]
            hid_ref[s, r0:r0 + mc, :] = (_silu(_dot(x, wg)) * _dot(x, wu)).astype(BF16)

    @pl.when(s >= nf)
    def _():
        wd = wd_ref[0].astype(BF16)
        lane = lax.broadcasted_iota(I32, (1, LANE), 1)
        for r0, mc in _row_chunks(rows):
            acc = _dot(hid_ref[0, r0:r0 + mc, :], wd[0:tf, :])
            for k in range(1, nf):
                acc = acc + _dot(hid_ref[k, r0:r0 + mc, :], wd[k * tf:(k + 1) * tf, :])
            gate = jnp.sum(jnp.where(lane == e, ga_ref[r0:r0 + mc, :], 0.0), axis=-1, keepdims=True)
            o_ref[r0:r0 + mc, :] = (acc * gate).astype(o_ref.dtype)


def experts(xs, gaff, w_gate, w_up, w_down, layer, tf, tn):
    _, E, D, F = w_gate.shape
    M = xs.shape[0]
    rows = M // E
    tf = min(tf, F)
    tn = min(tn, D)
    nf, nn = F // tf, D // tn
    return pl.pallas_call(
        functools.partial(_expert_kernel, nf=nf, tf=tf),
        grid=(E, nf + nn),
        in_specs=[pl.BlockSpec((rows, D), lambda e, s: (e, 0)),
                  pl.BlockSpec((None, 1, D, tf), lambda e, s: (layer, e, 0, jnp.minimum(s, nf - 1))),
                  pl.BlockSpec((None, 1, D, tf), lambda e, s: (layer, e, 0, jnp.minimum(s, nf - 1))),
                  pl.BlockSpec((None, 1, F, tn), lambda e, s: (layer, e, 0, jnp.maximum(s - nf, 0))),
                  pl.BlockSpec((rows, LANE), lambda e, s: (e, 0))],
        out_specs=pl.BlockSpec((rows, tn), lambda e, s: (e, jnp.maximum(s - nf, 0))),
        out_shape=jax.ShapeDtypeStruct((M, D), BF16),
        scratch_shapes=[pltpu.VMEM((nf, rows, tf), BF16)],
        compiler_params=_cp(("arbitrary", "arbitrary")),
        name="experts",
    )(xs, w_gate, w_up, w_down, gaff)


def _combine_kernel(st_ref, x_ref, g_ref, posq_ref, yg_hbm, o_ref, ssq_ref, ybuf, acc_ref, sem,
                    *, E, cap, B, nt):
    b = pl.program_id(0)
    t = pl.program_id(1)
    W = COMB_WIN
    KW = E * W
    step = b * nt + t
    slot = step % 2

    def tile_lo(bb, tt):
        base = (bb * (nt + 1) + tt) * E
        return [(st_ref[base + e] // BF16_ROWS) * BF16_ROWS for e in range(E)]

    def window_copy(bb, e, start, sl):
        row0 = pl.multiple_of((e * B + bb) * cap + start, BF16_ROWS)
        return pltpu.make_async_copy(yg_hbm.at[pl.ds(row0, W), :], ybuf.at[sl, pl.ds(e * W, W), :], sem.at[sl])

    def start_round0(bb, tt, sl):
        lo_n = tile_lo(bb, tt)
        for e in range(E):
            window_copy(bb, e, jnp.minimum(lo_n[e], cap - W), sl).start()

    @pl.when(step == 0)
    def _():
        start_round0(b, t, slot)

    @pl.when(step + 1 < B * nt)
    def _():
        start_round0((step + 1) // nt, (step + 1) % nt, 1 - slot)

    lo = tile_lo(b, t)
    base = (b * (nt + 1) + t) * E
    hi = [st_ref[base + E + e] for e in range(E)]
    nrounds = jnp.int32(1)
    for e in range(E):
        nrounds = jnp.maximum(nrounds, (hi[e] - lo[e] + (W - 1)) // W)

    qi = posq_ref[0].astype(I32)
    erow = lax.broadcasted_iota(I32, (LANE, KW), 0)
    ecol = lax.broadcasted_iota(I32, (LANE, KW), 1) // W
    expand = jnp.where(erow == ecol, 1.0, 0.0).astype(BF16)
    qe = (_dot((qi >> 5).astype(F32).astype(BF16), expand) * 32.0
          + _dot((qi & 31).astype(F32).astype(BF16), expand)).astype(I32) - 1
    lane_e = lax.broadcasted_iota(I32, (1, KW), 1) // W
    lane_k = lax.broadcasted_iota(I32, (1, KW), 1) % W

    def one_round(r, prefetched):
        lo_r = [lo[e] + r * W for e in range(E)]
        st = [jnp.minimum(lo_r[e], cap - W) for e in range(E)]
        if not prefetched:
            for e in range(E):
                window_copy(b, e, st[e], slot).start()
        stv = jnp.zeros((1, KW), I32)
        lov = jnp.zeros((1, KW), I32)
        for e in range(E):
            stv = jnp.where(lane_e == e, st[e], stv)
            lov = jnp.where(lane_e == e, lo_r[e], lov)
        onehot = (qe - stv == lane_k) & (qe >= lov) & (qe < lov + W)
        s = jnp.where(onehot, 1.0, 0.0).astype(BF16)
        for e in range(E):
            window_copy(b, e, st[e], slot).wait()
        return _dot(s, ybuf[slot])

    acc_ref[...] = one_round(0, True)

    def extra(r, carry):
        acc_ref[...] += one_round(r, False)
        return carry

    lax.fori_loop(1, nrounds, extra, 0)
    x2 = x_ref[...] + g_ref[0, pl.ds(b, 1), :] * acc_ref[...]
    o_ref[...] = x2
    ssq_ref[...] = jnp.sum(x2 * x2, axis=-1, keepdims=True)


def combine(starts_flat, x2d, mods, layer, posq, yg, cfg):
    B, N, D, E, cap = cfg.B, cfg.N, cfg.D, cfg.E, cfg.CAP
    nt = N // TOK_TILE
    assert cap >= COMB_WIN and cap % COMB_WIN == 0 and COMB_WIN % BF16_ROWS == 0 and cap <= 32 * 32
    return pl.pallas_call(
        functools.partial(_combine_kernel, E=E, cap=cap, B=B, nt=nt),
        grid_spec=pltpu.PrefetchScalarGridSpec(
            num_scalar_prefetch=1,
            grid=(B, nt),
            in_specs=[pl.BlockSpec((TOK_TILE, D), lambda b, t, st: (b * nt + t, 0)),
                      pl.BlockSpec((1, 8, D), lambda b, t, st: (layer, 0, 5)),
                      pl.BlockSpec((1, TOK_TILE, LANE), lambda b, t, st: (b, t, 0)),
                      pl.BlockSpec(memory_space=pl.ANY)],
            out_specs=[pl.BlockSpec((TOK_TILE, D), lambda b, t, st: (b * nt + t, 0)),
                       pl.BlockSpec((TOK_TILE, 1), lambda b, t, st: (b * nt + t, 0))],
            scratch_shapes=[pltpu.VMEM((2, E * COMB_WIN, D), BF16),
                            pltpu.VMEM((TOK_TILE, D), F32),
                            pltpu.SemaphoreType.DMA((2,))]),
        out_shape=[jax.ShapeDtypeStruct((B * N, D), F32),
                   jax.ShapeDtypeStruct((B * N, 1), F32)],
        compiler_params=_cp(("arbitrary", "arbitrary")),
        name="combine",
    )(starts_flat, x2d, mods, posq, yg)


def _pool_kernel(x_ref, ssq_ref, g_ref, sh_ref, sc_ref, o_ref, hs_ref, *, N, D, windows, chunks_per_group):
    b = pl.program_id(0)
    j = pl.program_id(1)
    P = max(windows) // 2
    cw = x_ref.shape[2]
    nblk = N // TOK_TILE
    g = g_ref[0]
    sh = sh_ref[0, pl.ds(b, 1), :]
    sc = sc_ref[0, pl.ds(b, 1), :]
    zeros = jnp.zeros((P, cw), F32)
    hs_ref[0:P, :] = zeros
    hs_ref[P + N:P + N + P, :] = zeros

    def fill(k, carry):
        r0 = pl.multiple_of(k * TOK_TILE, TOK_TILE)
        rstd = lax.rsqrt(ssq_ref[pl.ds(r0, TOK_TILE), :] * (1.0 / D) + NORM_EPS)
        y = x_ref[0, pl.ds(r0, TOK_TILE), :] * rstd * g
        hs_ref[pl.ds(pl.multiple_of(P + r0, 8), TOK_TILE), :] = y * (1.0 + sc) + sh
        return carry

    lax.fori_loop(0, nblk, fill, 0)

    for grp, w in enumerate(windows):
        half = w // 2

        @pl.when(j // chunks_per_group == grp)
        def _(half=half):
            def pool(k, carry):
                r0 = pl.multiple_of(k * TOK_TILE, TOK_TILE)
                big = hs_ref[pl.ds(r0, TOK_TILE + 2 * P), :]
                nrow = TOK_TILE + 2 * P
                fwd = big
                span = 1
                while span < 2 * half:
                    fwd = fwd + pltpu.roll(fwd, nrow - span, 0)
                    span *= 2
                acc = fwd[P - half:P - half + TOK_TILE]
                tk = lax.broadcasted_iota(I32, (TOK_TILE, 1), 0) + r0
                cnt = (jnp.minimum(tk + half, N) - jnp.maximum(tk - half, 0)).astype(F32)
                o_ref[0, pl.ds(r0, TOK_TILE), :] = (acc / cnt - big[P:P + TOK_TILE]).astype(o_ref.dtype)
                return carry

            lax.fori_loop(0, nblk, pool, 0)


def pool_mixer_input(x3, ssq, g3, mods, layer, cfg, cw):
    B, N, D = x3.shape
    dg = D // cfg.G
    cw = min(cw, dg)
    P = max(cfg.windows) // 2
    nj = D // cw

    def mspec(chunk):
        return pl.BlockSpec((1, 8, cw), lambda b, j: (layer, 0, chunk * nj + j))

    return pl.pallas_call(
        functools.partial(_pool_kernel, N=N, D=D, windows=cfg.windows, chunks_per_group=dg // cw),
        grid=(B, nj),
        in_specs=[pl.BlockSpec((1, N, cw), lambda b, j: (b, 0, j)),
                  pl.BlockSpec((N, 1), lambda b, j: (b, 0)),
                  pl.BlockSpec((1, 1, cw), lambda b, j: (layer, 0, j)),
                  mspec(0), mspec(1)],
        out_specs=pl.BlockSpec((1, N, cw), lambda b, j: (b, 0, j)),
        out_shape=jax.ShapeDtypeStruct((B, N, D), BF16),
        scratch_shapes=[pltpu.VMEM((N + 2 * P, cw), F32)],
        compiler_params=_cp(("parallel", "parallel")),
        name="pool",
    )(x3, ssq, g3, mods, mods)


def moe_block(x2d, mods, layer, g_ffn3, w_router, w_gate, w_up, w_down, cfg):
    B, N, D, E, cap = cfg.B, cfg.N, cfg.D, cfg.E, cfg.CAP
    nt = N // TOK_TILE
    wr_pad = jnp.pad(w_router, ((0, 0), (0, LANE - E)))
    hfx = ffn_norm_router(x2d.reshape(B, N, D), g_ffn3, mods, layer, wr_pad, E, tm=512)
    idx, posq, starts = route(hfx, cfg)
    idx_flat = jnp.transpose(idx, (1, 0, 2)).reshape(-1)
    starts_flat = jnp.transpose(starts[:, :, :nt + 1], (0, 2, 1)).reshape(-1)
    xs, gaff = gather_rows(idx_flat, hfx.reshape(B * N, D + LANE), R=min(256, cap))
    yg = experts(xs, gaff, w_gate, w_up, w_down, layer, tf=256, tn=512)
    return combine(starts_flat, x2d, mods, layer, posq, yg, cfg)


def forward(cfg, x, c, ctx, c_ctx, ada_w, ada_b, norm_mix_g, norm_ffn_g, na_w_qkv, na_q_gain,
            na_k_gain, na_rpb, na_w_out, pool_w, pool_scale, moe_w_router, moe_w_gate,
            moe_w_up, moe_w_down):
    B, N, D, NC, H = cfg.B, cfg.N, cfg.D, cfg.NC, cfg.H
    dh = D // H
    L = ada_w.shape[0]
    assert L == 2 and B + 1 <= 8
    cond8 = jnp.concatenate([c, c_ctx[None, :], jnp.zeros((8 - B - 1, D), F32)], axis=0)
    mods = ada_mods(cond8, ada_w, ada_b)
    gmix3 = norm_mix_g.reshape(L, 1, D)
    gffn3 = norm_ffn_g.reshape(L, 1, D)

    h = normmod(x, gmix3, mods, 0, 0, None, tm=512).reshape(B * N, D)
    hc = normmod(ctx, gmix3, mods, 0, 0, B, tm=512).reshape(B * NC, D)
    scale = 1.0 / math.sqrt(dh)
    gain = jnp.concatenate([jnp.tile(na_q_gain[0] * scale, H), jnp.tile(na_k_gain[0], H),
                            jnp.ones((D,), F32)])[None, :]
    w_qkv = na_w_qkv[0]
    qk = qkv_proj(h, w_qkv, gain, 0, 2 * D, dh, True, tm=2048, tn=256)
    v = qkv_proj(h, w_qkv, gain, 2 * D, D, dh, False, tm=2048, tn=256)
    kc = qkv_proj(hc, w_qkv, gain, D, D, dh, True, tm=1024, tn=256)
    vc = qkv_proj(hc, w_qkv, gain, 2 * D, D, dh, False, tm=1024, tn=256)
    rpb_pad = jnp.pad(na_rpb[0], ((0, 0), (0, 1), (0, LANE - (2 * NA_KW - 1))))
    o = attention(qk, v, kc, vc, rpb_pad, cfg)
    x1 = proj_residual(o, na_w_out[0][None], x.reshape(B * N, D), mods, 0, 2, N, None, tm=2048, tn=256)
    x2, ssq = moe_block(x1, mods, 0, gffn3, moe_w_router[0], moe_w_gate, moe_w_up, moe_w_down, cfg)

    pooled = pool_mixer_input(x2.reshape(B, N, D), ssq, gmix3, mods, 1, cfg, cw=256).reshape(B * N, D)
    x3 = proj_residual(pooled, pool_w[0], x2, mods, 1, 2, N, pool_scale[0][None, :], tm=1024, tn=1024)
    x4, _ = moe_block(x3, mods, 1, gffn3, moe_w_router[1], moe_w_gate, moe_w_up, moe_w_down, cfg)
    return x4.reshape(B, N, D)


def kernel(x, c, ctx, c_ctx, ada_w, ada_b, norm_mix_g, norm_ffn_g, na_w_qkv, na_q_gain, na_k_gain, na_rpb, na_w_out, pool_w, pool_scale, moe_w_router, moe_w_gate, moe_w_up, moe_w_down):
    B, N, D = x.shape
    E, _, F = moe_w_gate.shape[1:]
    G = pool_w.shape[1]
    cfg = Cfg(B=B, N=N, D=D, NC=ctx.shape[1], H=na_rpb.shape[1], GW=64, E=E,
              CAP=max(1, 2 * N // E), F=F, G=G, windows=(2, 4, 8, 16), n_ada=6)
    return forward(cfg, x, c, ctx, c_ctx, ada_w, ada_b, norm_mix_g, norm_ffn_g, na_w_qkv, na_q_gain,
                   na_k_gain, na_rpb, na_w_out, pool_w, pool_scale, moe_w_router, moe_w_gate,
                   moe_w_up, moe_w_down)
```

```python
import functools
import math
from typing import NamedTuple

import jax
import jax.numpy as jnp
from jax import lax
from jax.experimental import pallas as pl
from jax.experimental.pallas import tpu as pltpu

F32 = jnp.float32
BF16 = jnp.bfloat16
I32 = jnp.int32
U32 = jnp.uint32

LANE = 128
NORM_EPS = 1e-6
NEG_INF = -1e30
NA_KH = 8
NA_KW = 16
ATTN_QROWS = 4
ATTN_KROWS = ATTN_QROWS + NA_KH
TOK_TILE = 256
COMB_WIN = 64
BF16_ROWS = 16
MM_CHUNK = 512
EXPERT_KSPLIT = 2
VMEM_LIMIT = 56 * 1024 * 1024


class Cfg(NamedTuple):
    B: int
    N: int
    D: int
    NC: int
    H: int
    GW: int
    E: int
    CAP: int
    F: int
    G: int
    windows: tuple
    n_ada: int


def _cp(sem, vmem=VMEM_LIMIT):
    return pltpu.CompilerParams(dimension_semantics=sem, vmem_limit_bytes=vmem)


def _dot(a, b):
    return jnp.dot(a, b, preferred_element_type=F32)


def _dot_nt(a, b):
    return lax.dot_general(a, b, (((1,), (1,)), ((), ())), preferred_element_type=F32)


def _silu(x):
    return x * (1.0 / (1.0 + jnp.exp(-x)))


def _row_chunks(tm):
    mc = min(MM_CHUNK, tm)
    return [(r, mc) for r in range(0, tm, mc)]


def _ada_kernel(cond_ref, w_ref, b_ref, o_ref):
    s = _silu(cond_ref[...]).astype(BF16)
    o_ref[0] = _dot(s, w_ref[0].astype(BF16)) + b_ref[0]


def ada_mods(cond8, ada_w, ada_b):
    L, D, ND = ada_w.shape
    tn = min(512, ND)
    return pl.pallas_call(
        _ada_kernel,
        grid=(L, ND // tn),
        in_specs=[pl.BlockSpec((8, D), lambda l, j: (0, 0)),
                  pl.BlockSpec((1, D, tn), lambda l, j: (l, 0, j)),
                  pl.BlockSpec((1, 1, tn), lambda l, j: (l, 0, j))],
        out_specs=pl.BlockSpec((1, 8, tn), lambda l, j: (l, 0, j)),
        out_shape=jax.ShapeDtypeStruct((L, 8, ND), F32),
        compiler_params=_cp(("parallel", "parallel")),
        name="ada",
    )(cond8, ada_w, ada_b.reshape(L, 1, ND))


def _norm_mod(x, g, sh, sc):
    ms = jnp.mean(x * x, axis=-1, keepdims=True)
    y = x * lax.rsqrt(ms + NORM_EPS) * g
    return y * (1.0 + sc) + sh


def _normmod_kernel(x_ref, g_ref, sh_ref, sc_ref, o_ref, *, row):
    r = pl.program_id(0) if row is None else row
    sh = sh_ref[0, pl.ds(r, 1), :]
    sc = sc_ref[0, pl.ds(r, 1), :]
    o_ref[0] = _norm_mod(x_ref[0], g_ref[0], sh, sc).astype(o_ref.dtype)


def _mod_spec(D, layer, chunk):
    return pl.BlockSpec((1, 8, D), lambda b, i: (layer, 0, chunk))


def normmod(x3, g3, mods, layer, sh_chunk, row, tm):
    B, n, D = x3.shape
    tm = min(tm, n)
    return pl.pallas_call(
        functools.partial(_normmod_kernel, row=row),
        grid=(B, n // tm),
        in_specs=[pl.BlockSpec((1, tm, D), lambda b, i: (b, i, 0)),
                  pl.BlockSpec((1, 1, D), lambda b, i: (layer, 0, 0)),
                  _mod_spec(D, layer, sh_chunk),
                  _mod_spec(D, layer, sh_chunk + 1)],
        out_specs=pl.BlockSpec((1, tm, D), lambda b, i: (b, i, 0)),
        out_shape=jax.ShapeDtypeStruct((B, n, D), BF16),
        compiler_params=_cp(("parallel", "parallel")),
        name="normmod",
    )(x3, g3, mods, mods)


def _qkv_kernel(a_ref, w_ref, gain_ref, o_ref, *, norm, dh):
    wb = w_ref[...].astype(BF16)
    tn = wb.shape[1]
    for r0, mc in _row_chunks(a_ref.shape[0]):
        acc = _dot(a_ref[r0:r0 + mc, :], wb)
        for c in range(tn // dh):
            y = acc[:, c * dh:(c + 1) * dh]
            if norm:
                ms = jnp.mean(y * y, axis=-1, keepdims=True)
                y = y * lax.rsqrt(ms + NORM_EPS) * gain_ref[:, c * dh:(c + 1) * dh]
            o_ref[c, r0:r0 + mc, :] = y.astype(o_ref.dtype)


def qkv_proj(a, w, gain, col0, ncols, dh, norm, tm, tn):
    M, D = a.shape
    tm = min(tm, M)
    off = col0 // tn
    return pl.pallas_call(
        functools.partial(_qkv_kernel, norm=norm, dh=dh),
        grid=(M // tm, ncols // tn),
        in_specs=[pl.BlockSpec((tm, D), lambda i, j: (i, 0)),
                  pl.BlockSpec((D, tn), lambda i, j: (0, j + off)),
                  pl.BlockSpec((1, tn), lambda i, j: (0, j + off))],
        out_specs=pl.BlockSpec((tn // dh, tm, dh), lambda i, j: (j, i, 0)),
        out_shape=jax.ShapeDtypeStruct((ncols // dh, M, dh), BF16),
        compiler_params=_cp(("parallel", "arbitrary")),
        name="qkv",
    )(a, w, gain)


def _attn_bias_variants():
    out = []
    for var in range(3):
        tab = {}
        for i in range(ATTN_QROWS):
            for j in range(ATTN_KROWS):
                if var == 0:
                    valid, dr = j < NA_KH, j - i + NA_KH - 1
                elif var == 1:
                    valid, dr = i <= j < i + NA_KH, j - i + NA_KH // 2 - 1
                else:
                    valid, dr = j >= ATTN_KROWS - NA_KH, j - i - 1
                tab[(i, j)] = (valid, dr)
        out.append(tab)
    return out


def _attn_kernel(rpb_ref, q_ref, k_ref, v_ref, kc_ref, vc_ref, o_ref, bias_ref, sl_ref, sc_ref,
                 pl_ref, pc_ref, den_ref, *, gw, rows):
    nblk = rows // ATTN_QROWS
    qn = ATTN_QROWS * gw
    kn = ATTN_KROWS * gw

    @pl.when(pl.program_id(1) == 0)
    def _build_bias():
        qc = lax.broadcasted_iota(I32, (gw, gw), 0)
        kc = lax.broadcasted_iota(I32, (gw, gw), 1)
        cs = jnp.clip(qc - NA_KW // 2, 0, gw - NA_KW)
        col_mask = jnp.where((kc >= cs) & (kc < cs + NA_KW), 0.0, NEG_INF).astype(F32)
        tiles = []
        for dr in range(2 * NA_KH - 1):
            r = jnp.broadcast_to(rpb_ref[0, dr:dr + 1, :], (gw, LANE))
            t = pltpu.roll(r, LANE - (NA_KW - 1), 1, stride=1, stride_axis=0)
            tiles.append(t[:, :gw] + col_mask)
        neg = jnp.full((gw, gw), NEG_INF, F32)
        for var, tab in enumerate(_attn_bias_variants()):
            for (i, j), (valid, dr) in tab.items():
                bias_ref[var, i * gw:(i + 1) * gw, j * gw:(j + 1) * gw] = tiles[dr] if valid else neg

    def geom(blk):
        if isinstance(blk, int):
            r0 = blk * ATTN_QROWS
            start = min(max(r0 - NA_KH // 2, 0), rows - ATTN_KROWS)
            var = 0 if blk == 0 else (2 if blk == nblk - 1 else 1)
            return r0 * gw, start * gw, var
        r0 = blk * ATTN_QROWS
        start = jnp.clip(r0 - NA_KH // 2, 0, rows - ATTN_KROWS)
        var = jnp.where(blk == 0, 0, jnp.where(blk == nblk - 1, 2, 1))
        return pl.multiple_of(r0 * gw, qn), pl.multiple_of(start * gw, gw), var

    def scores(blk, slot):
        q0, k0, var = geom(blk)
        q = q_ref[pl.ds(q0, qn), :]
        sl_ref[slot] = _dot_nt(q, k_ref[pl.ds(k0, kn), :]) + bias_ref[var]
        sc_ref[slot] = _dot_nt(q, kc_ref[...])

    def softmax(slot):
        s_lat = sl_ref[slot]
        s_ctx = sc_ref[slot]
        m = jnp.maximum(jnp.max(s_lat, axis=-1, keepdims=True), jnp.max(s_ctx, axis=-1, keepdims=True))
        p_lat = jnp.exp(s_lat - m)
        p_ctx = jnp.exp(s_ctx - m)
        den_ref[slot] = jnp.sum(p_lat, axis=-1, keepdims=True) + jnp.sum(p_ctx, axis=-1, keepdims=True)
        pl_ref[slot] = p_lat.astype(BF16)
        pc_ref[slot] = p_ctx.astype(BF16)

    def values(blk, slot):
        q0, k0, _ = geom(blk)
        o = _dot(pl_ref[slot], v_ref[pl.ds(k0, kn), :]) + _dot(pc_ref[slot], vc_ref[...])
        o_ref[pl.ds(q0, qn), :] = (o * (1.0 / den_ref[slot])).astype(o_ref.dtype)

    scores(0, 0)
    scores(1, 1)
    softmax(0)

    def body(it, carry):
        i = 2 + 2 * it
        scores(i, 0)
        softmax(1)
        values(i - 2, 0)
        scores(i + 1, 1)
        softmax(0)
        values(i - 1, 1)
        return carry

    lax.fori_loop(0, (nblk - 2) // 2, body, 0)
    softmax(1)
    values(nblk - 2, 0)
    values(nblk - 1, 1)


def attention(qk, v, kc, vc, rpb_pad, cfg):
    B, N, D, NC, H, GW = cfg.B, cfg.N, cfg.D, cfg.NC, cfg.H, cfg.GW
    dh = D // H
    rows = N // GW
    assert dh == LANE and rows >= ATTN_KROWS and rows % (2 * ATTN_QROWS) == 0 and GW >= NA_KW
    return pl.pallas_call(
        functools.partial(_attn_kernel, gw=GW, rows=rows),
        grid=(H, B),
        in_specs=[pl.BlockSpec((1, 2 * NA_KH, LANE), lambda h, b: (h, 0, 0)),
                  pl.BlockSpec((None, N, dh), lambda h, b: (h, b, 0)),
                  pl.BlockSpec((None, N, dh), lambda h, b: (H + h, b, 0)),
                  pl.BlockSpec((None, N, dh), lambda h, b: (h, b, 0)),
                  pl.BlockSpec((None, NC, dh), lambda h, b: (h, b, 0)),
                  pl.BlockSpec((None, NC, dh), lambda h, b: (h, b, 0))],
        out_specs=pl.BlockSpec((N, dh), lambda h, b: (b, h)),
        out_shape=jax.ShapeDtypeStruct((B * N, D), BF16),
        scratch_shapes=[pltpu.VMEM((3, ATTN_QROWS * GW, ATTN_KROWS * GW), F32),
                        pltpu.VMEM((2, ATTN_QROWS * GW, ATTN_KROWS * GW), F32),
                        pltpu.VMEM((2, ATTN_QROWS * GW, NC), F32),
                        pltpu.VMEM((2, ATTN_QROWS * GW, ATTN_KROWS * GW), BF16),
                        pltpu.VMEM((2, ATTN_QROWS * GW, NC), BF16),
                        pltpu.VMEM((2, ATTN_QROWS * GW, 1), F32)],
        compiler_params=_cp(("arbitrary", "arbitrary")),
        name="attn",
    )(rpb_pad, qk, qk, v, kc, vc)


def _proj_res_kernel(a_ref, w_ref, x_ref, g_ref, *rest, tiles_per_batch, scaled):
    if scaled:
        ps_ref, o_ref = rest
    else:
        (o_ref,) = rest
    wb = w_ref[0].astype(BF16)
    b = pl.program_id(0) // tiles_per_batch
    g = g_ref[0, pl.ds(b, 1), :]
    for r0, mc in _row_chunks(a_ref.shape[0]):
        y = _dot(a_ref[r0:r0 + mc, :], wb)
        if scaled:
            y = y * ps_ref[...]
        o_ref[r0:r0 + mc, :] = x_ref[r0:r0 + mc, :] + g * y


def proj_residual(a, w3, x2, mods, layer, gate_chunk, n_per_batch, pscale, tm, tn):
    M = a.shape[0]
    G, K, KO = w3.shape
    Dout = G * KO
    tm = min(tm, n_per_batch)
    tn = min(tn, KO)
    nj = KO // tn
    scaled = pscale is not None
    in_specs = [pl.BlockSpec((tm, K), lambda i, g, j: (i, g)),
                pl.BlockSpec((1, K, tn), lambda i, g, j: (g, 0, j)),
                pl.BlockSpec((tm, tn), lambda i, g, j: (i, g * nj + j)),
                pl.BlockSpec((1, 8, tn), lambda i, g, j: (layer, 0, gate_chunk * (Dout // tn) + g * nj + j))]
    args = [a, w3, x2, mods]
    if scaled:
        in_specs.append(pl.BlockSpec((1, tn), lambda i, g, j: (0, g * nj + j)))
        args.append(pscale)
    return pl.pallas_call(
        functools.partial(_proj_res_kernel, tiles_per_batch=n_per_batch // tm, scaled=scaled),
        grid=(M // tm, G, nj),
        in_specs=in_specs,
        out_specs=pl.BlockSpec((tm, tn), lambda i, g, j: (i, g * nj + j)),
        out_shape=jax.ShapeDtypeStruct((M, Dout), F32),
        compiler_params=_cp(("parallel", "arbitrary", "arbitrary")),
        name="proj_res",
    )(*args)


def _split_bf16(x):
    hi = x.astype(BF16)
    lo = (x - hi.astype(F32)).astype(BF16)
    return hi, lo


def _ffn_norm_kernel(x_ref, g_ref, sh_ref, sc_ref, wr_ref, hfx_ref, *, E):
    b = pl.program_id(0)
    D = x_ref.shape[2]
    hf = _norm_mod(x_ref[0], g_ref[0], sh_ref[0, pl.ds(b, 1), :], sc_ref[0, pl.ds(b, 1), :])
    bits = pltpu.bitcast(hf.astype(BF16).astype(F32), U32)
    hfx_ref[0, :, 0:D // 2] = (bits[:, 0:D // 2] >> 16) | bits[:, D // 2:D]
    h_hi, h_lo = _split_bf16(hf)
    w_hi, w_lo = _split_bf16(wr_ref[...])
    lg = _dot(h_hi, w_hi) + (_dot(h_hi, w_lo) + _dot(h_lo, w_hi))
    lane = lax.broadcasted_iota(I32, (1, LANE), 1)
    lg = jnp.where(lane < E, lg, NEG_INF)
    ex = jnp.exp(lg - jnp.max(lg, axis=-1, keepdims=True))
    aff = ex / jnp.sum(ex, axis=-1, keepdims=True)
    hfx_ref[0, :, D // 2:D // 2 + LANE] = pltpu.bitcast(aff, U32)


def ffn_norm_router(x3, g3, mods, layer, wr_pad, E, tm):
    B, N, D = x3.shape
    tm = min(tm, N)
    return pl.pallas_call(
        functools.partial(_ffn_norm_kernel, E=E),
        grid=(B, N // tm),
        in_specs=[pl.BlockSpec((1, tm, D), lambda b, i: (b, i, 0)),
                  pl.BlockSpec((1, 1, D), lambda b, i: (layer, 0, 0)),
                  _mod_spec(D, layer, 3),
                  _mod_spec(D, layer, 4),
                  pl.BlockSpec((D, LANE), lambda b, i: (0, 0))],
        out_specs=pl.BlockSpec((1, tm, D // 2 + LANE), lambda b, i: (b, i, 0)),
        out_shape=jax.ShapeDtypeStruct((B, N, D // 2 + LANE), U32),
        compiler_params=_cp(("parallel", "parallel")),
        name="ffn_norm",
    )(x3, g3, mods, mods, wr_pad)


def _route_kernel(aff_ref, idx_ref, posq_ref, starts_ref, t_ref, cum_ref, *, E, cap, N):
    b = pl.program_id(0)
    nblk = N // TOK_TILE
    logn = int(math.log2(N))
    for k in range(N // LANE):
        t_ref[:, k * LANE:(k + 1) * LANE] = pltpu.bitcast(aff_ref[0, k * LANE:(k + 1) * LANE, :], F32).T
    bits = pltpu.bitcast(t_ref[0:E, :], I32)
    tok = lax.broadcasted_iota(I32, (1, N), 1)

    def count(pred):
        return jnp.sum(jnp.where(pred, 1, 0), axis=1, keepdims=True)

    def thr_step(i, thr):
        cand = thr | (jnp.int32(1) << (30 - i))
        return jnp.where(count(bits >= cand) >= cap, cand, thr)

    thr = lax.fori_loop(0, 31, thr_step, jnp.zeros((E, 1), I32))
    gt = bits > thr
    eq = bits == thr
    need = cap - count(gt)

    def tie_step(i, ans):
        cand = ans | (jnp.int32(1) << (logn - 1 - i))
        return jnp.where(count(eq & (tok < cand)) < need, cand, ans)

    last = lax.fori_loop(0, logn, tie_step, jnp.zeros((E, 1), I32))
    sel = gt | (eq & (tok <= last))
    sel_f = jnp.where(sel, 1.0, 0.0).astype(F32)

    ri = lax.broadcasted_iota(I32, (TOK_TILE, TOK_TILE), 0)
    ci = lax.broadcasted_iota(I32, (TOK_TILE, TOK_TILE), 1)
    tri = jnp.where(ri <= ci, 1.0, 0.0).astype(BF16)
    lane_s = lax.broadcasted_iota(I32, (1, LANE), 1)
    carry = jnp.zeros((E, 1), F32)
    starts = jnp.zeros((E, LANE), F32)
    for k in range(nblk):
        m = sel_f[:, k * TOK_TILE:(k + 1) * TOK_TILE]
        starts = jnp.where(lane_s == k, carry, starts)
        cum_ref[:, k * TOK_TILE:(k + 1) * TOK_TILE] = _dot(m.astype(BF16), tri) + carry
        carry = carry + jnp.sum(m, axis=1, keepdims=True)
    starts_ref[0] = jnp.where(lane_s == nblk, carry, starts).astype(I32)

    lane_c = lax.broadcasted_iota(I32, (1, cap), 1)

    def slot_step(c, out):
        cnt = jnp.sum(jnp.where(cum_ref[...] <= lax.convert_element_type(c, F32), 1, 0), axis=1, keepdims=True)
        return jnp.where(lane_c == c, cnt, out)

    idx = lax.fori_loop(0, cap, slot_step, jnp.zeros((E, cap), I32), unroll=4)
    idx_ref[0] = idx + b * N

    t_ref[0:E, :] = jnp.where(sel, cum_ref[...], 0.0)
    t_ref[E:LANE, :] = jnp.zeros((LANE - E, N), F32)
    for k in range(N // LANE):
        posq_ref[0, k * LANE:(k + 1) * LANE, :] = t_ref[:, k * LANE:(k + 1) * LANE].T


def route(hfx, cfg):
    B, N, D, E, cap = cfg.B, cfg.N, cfg.D, cfg.E, cfg.CAP
    nblk = N // TOK_TILE
    assert N % TOK_TILE == 0 and cap % LANE == 0 and (1 << int(math.log2(N))) == N
    assert E % 8 == 0 and nblk < LANE
    return pl.pallas_call(
        functools.partial(_route_kernel, E=E, cap=cap, N=N),
        grid=(B,),
        in_specs=[pl.BlockSpec((1, N, LANE), lambda b: (b, 0, D // 2 // LANE))],
        out_specs=[pl.BlockSpec((1, E, cap), lambda b: (b, 0, 0)),
                   pl.BlockSpec((1, N, LANE), lambda b: (b, 0, 0)),
                   pl.BlockSpec((1, E, LANE), lambda b: (b, 0, 0))],
        out_shape=[jax.ShapeDtypeStruct((B, E, cap), I32),
                   jax.ShapeDtypeStruct((B, N, LANE), F32),
                   jax.ShapeDtypeStruct((B, E, LANE), I32)],
        scratch_shapes=[pltpu.VMEM((LANE, N), F32), pltpu.VMEM((E, N), F32)],
        compiler_params=_cp(("arbitrary",)),
        name="route",
    )(hfx)


def _gather_kernel(idx_ref, hfx_hbm, xs_ref, ga_ref, buf, sem, *, R, D):
    i = pl.program_id(0)
    n = pl.num_programs(0)

    def issue(step, slot):
        for r in range(R):
            tok = idx_ref[step * R + r]
            pltpu.make_async_copy(hfx_hbm.at[pl.ds(tok, 1), :], buf.at[slot, pl.ds(r, 1), :],
                                  sem.at[slot]).start()

    @pl.when(i == 0)
    def _():
        issue(0, 0)

    @pl.when(i + 1 < n)
    def _():
        issue(i + 1, (i + 1) % 2)

    slot = i % 2
    pltpu.make_async_copy(hfx_hbm.at[pl.ds(0, R), :], buf.at[slot], sem.at[slot]).wait()
    w = buf[slot, :, 0:D // 2]
    xs_ref[:, 0:D // 2] = pltpu.bitcast(w << 16, F32).astype(xs_ref.dtype)
    xs_ref[:, D // 2:D] = pltpu.bitcast(w & jnp.uint32(0xFFFF0000), F32).astype(xs_ref.dtype)
    ga_ref[...] = pltpu.bitcast(buf[slot, :, D // 2:D // 2 + LANE], F32)


def gather_rows(idx_flat, hfx2, R):
    M = idx_flat.shape[0]
    DX = hfx2.shape[1]
    D = 2 * (DX - LANE)
    return pl.pallas_call(
        functools.partial(_gather_kernel, R=R, D=D),
        grid_spec=pltpu.PrefetchScalarGridSpec(
            num_scalar_prefetch=1,
            grid=(M // R,),
            in_specs=[pl.BlockSpec(memory_space=pl.ANY)],
            out_specs=[pl.BlockSpec((R, D), lambda i, idx: (i, 0)),
                       pl.BlockSpec((R, LANE), lambda i, idx: (i, 0))],
            scratch_shapes=[pltpu.VMEM((2, R, DX), U32), pltpu.SemaphoreType.DMA((2,))]),
        out_shape=[jax.ShapeDtypeStruct((M, D), BF16), jax.ShapeDtypeStruct((M, LANE), F32)],
        compiler_params=_cp(("arbitrary",)),
        name="gather",
    )(idx_flat, hfx2)


def _expert_kernel(xs_ref, wg_ref, wu_ref, wd_ref, ga_ref, o_ref, hid_ref, ag_ref, au_ref, *, nf, nk, tf):
    e = pl.program_id(0)
    s = pl.program_id(1)
    rows = xs_ref.shape[0]
    dk = xs_ref.shape[1] // nk

    for kh in range(nk):
        @pl.when((s < nf * nk) & (s % nk == kh))
        def _(kh=kh):
            wg = wg_ref[0].astype(BF16)
            wu = wu_ref[0].astype(BF16)
            for r0, mc in _row_chunks(rows):
                x = xs_ref[r0:r0 + mc, kh * dk:(kh + 1) * dk]
                a = _dot(x, wg)
                u = _dot(x, wu)
                if kh > 0:
                    a = a + ag_ref[r0:r0 + mc, :]
                    u = u + au_ref[r0:r0 + mc, :]
                if kh < nk - 1:
                    ag_ref[r0:r0 + mc, :] = a
                    au_ref[r0:r0 + mc, :] = u
                else:
                    hid_ref[s // nk, r0:r0 + mc, :] = (_silu(a) * u).astype(BF16)

    @pl.when(s >= nf * nk)
    def _():
        wd = wd_ref[0].astype(BF16)
        lane = lax.broadcasted_iota(I32, (1, LANE), 1)
        for r0, mc in _row_chunks(rows):
            acc = _dot(hid_ref[0, r0:r0 + mc, :], wd[0:tf, :])
            for k in range(1, nf):
                acc = acc + _dot(hid_ref[k, r0:r0 + mc, :], wd[k * tf:(k + 1) * tf, :])
            gate = jnp.sum(jnp.where(lane == e, ga_ref[r0:r0 + mc, :], 0.0), axis=-1, keepdims=True)
            o_ref[r0:r0 + mc, :] = (acc * gate).astype(o_ref.dtype)


def experts(xs, gaff, w_gate, w_up, w_down, layer, tf, tn):
    _, E, D, F = w_gate.shape
    M = xs.shape[0]
    rows = M // E
    tf = min(tf, F)
    tn = min(tn, D)
    nf, nn = F // tf, D // tn
    nk = EXPERT_KSPLIT
    n1 = nf * nk

    def w_in(e, s):
        s1 = jnp.minimum(s, n1 - 1)
        return (layer, e, s1 % nk, s1 // nk)

    return pl.pallas_call(
        functools.partial(_expert_kernel, nf=nf, nk=nk, tf=tf),
        grid=(E, n1 + nn),
        in_specs=[pl.BlockSpec((rows, D), lambda e, s: (e, 0)),
                  pl.BlockSpec((None, 1, D // nk, tf), w_in),
                  pl.BlockSpec((None, 1, D // nk, tf), w_in),
                  pl.BlockSpec((None, 1, F, tn), lambda e, s: (layer, e, 0, jnp.maximum(s - n1, 0))),
                  pl.BlockSpec((rows, LANE), lambda e, s: (e, 0))],
        out_specs=pl.BlockSpec((rows, tn), lambda e, s: (e, jnp.maximum(s - n1, 0))),
        out_shape=jax.ShapeDtypeStruct((M, D), BF16),
        scratch_shapes=[pltpu.VMEM((nf, rows, tf), BF16),
                        pltpu.VMEM((rows, tf), F32), pltpu.VMEM((rows, tf), F32)],
        compiler_params=_cp(("arbitrary", "arbitrary")),
        name="experts",
    )(xs, w_gate, w_up, w_down, gaff)


def _combine_kernel(st_ref, x_ref, g_ref, posq_ref, yg_hbm, o_ref, ssq_ref, ybuf, acc_ref, sem,
                    *, E, cap, B, nt):
    b = pl.program_id(0)
    t = pl.program_id(1)
    W = COMB_WIN
    KW = E * W
    step = b * nt + t
    slot = step % 2

    def tile_lo(bb, tt):
        base = (bb * (nt + 1) + tt) * E
        return [(st_ref[base + e] // BF16_ROWS) * BF16_ROWS for e in range(E)]

    def window_copy(bb, e, start, sl):
        row0 = pl.multiple_of((e * B + bb) * cap + start, BF16_ROWS)
        return pltpu.make_async_copy(yg_hbm.at[pl.ds(row0, W), :], ybuf.at[sl, pl.ds(e * W, W), :], sem.at[sl])

    def start_round0(bb, tt, sl):
        lo_n = tile_lo(bb, tt)
        for e in range(E):
            window_copy(bb, e, jnp.minimum(lo_n[e], cap - W), sl).start()

    @pl.when(step == 0)
    def _():
        start_round0(b, t, slot)

    @pl.when(step + 1 < B * nt)
    def _():
        start_round0((step + 1) // nt, (step + 1) % nt, 1 - slot)

    lo = tile_lo(b, t)
    base = (b * (nt + 1) + t) * E
    hi = [st_ref[base + E + e] for e in range(E)]
    nrounds = jnp.int32(1)
    for e in range(E):
        nrounds = jnp.maximum(nrounds, (hi[e] - lo[e] + (W - 1)) // W)

    qi = posq_ref[0].astype(I32)
    erow = lax.broadcasted_iota(I32, (LANE, KW), 0)
    ecol = lax.broadcasted_iota(I32, (LANE, KW), 1) // W
    expand = jnp.where(erow == ecol, 1.0, 0.0).astype(BF16)
    qe = (_dot((qi >> 5).astype(F32).astype(BF16), expand) * 32.0
          + _dot((qi & 31).astype(F32).astype(BF16), expand)).astype(I32) - 1
    lane_e = lax.broadcasted_iota(I32, (1, KW), 1) // W
    lane_k = lax.broadcasted_iota(I32, (1, KW), 1) % W

    def one_round(r, prefetched):
        lo_r = [lo[e] + r * W for e in range(E)]
        st = [jnp.minimum(lo_r[e], cap - W) for e in range(E)]
        if not prefetched:
            for e in range(E):
                window_copy(b, e, st[e], slot).start()
        stv = jnp.zeros((1, KW), I32)
        lov = jnp.zeros((1, KW), I32)
        for e in range(E):
            stv = jnp.where(lane_e == e, st[e], stv)
            lov = jnp.where(lane_e == e, lo_r[e], lov)
        onehot = (qe - stv == lane_k) & (qe >= lov) & (qe < lov + W)
        s = jnp.where(onehot, 1.0, 0.0).astype(BF16)
        for e in range(E):
            window_copy(b, e, st[e], slot).wait()
        return _dot(s, ybuf[slot])

    acc_ref[...] = one_round(0, True)

    def extra(r, carry):
        acc_ref[...] += one_round(r, False)
        return carry

    lax.fori_loop(1, nrounds, extra, 0)
    x2 = x_ref[...] + g_ref[0, pl.ds(b, 1), :] * acc_ref[...]
    o_ref[...] = x2
    ssq_ref[...] = jnp.sum(x2 * x2, axis=-1, keepdims=True)


def combine(starts_flat, x2d, mods, layer, posq, yg, cfg):
    B, N, D, E, cap = cfg.B, cfg.N, cfg.D, cfg.E, cfg.CAP
    nt = N // TOK_TILE
    assert cap >= COMB_WIN and cap % COMB_WIN == 0 and COMB_WIN % BF16_ROWS == 0 and cap <= 32 * 32
    return pl.pallas_call(
        functools.partial(_combine_kernel, E=E, cap=cap, B=B, nt=nt),
        grid_spec=pltpu.PrefetchScalarGridSpec(
            num_scalar_prefetch=1,
            grid=(B, nt),
            in_specs=[pl.BlockSpec((TOK_TILE, D), lambda b, t, st: (b * nt + t, 0)),
                      pl.BlockSpec((1, 8, D), lambda b, t, st: (layer, 0, 5)),
                      pl.BlockSpec((1, TOK_TILE, LANE), lambda b, t, st: (b, t, 0)),
                      pl.BlockSpec(memory_space=pl.ANY)],
            out_specs=[pl.BlockSpec((TOK_TILE, D), lambda b, t, st: (b * nt + t, 0)),
                       pl.BlockSpec((TOK_TILE, 1), lambda b, t, st: (b * nt + t, 0))],
            scratch_shapes=[pltpu.VMEM((2, E * COMB_WIN, D), BF16),
                            pltpu.VMEM((TOK_TILE, D), F32),
                            pltpu.SemaphoreType.DMA((2,))]),
        out_shape=[jax.ShapeDtypeStruct((B * N, D), F32),
                   jax.ShapeDtypeStruct((B * N, 1), F32)],
        compiler_params=_cp(("arbitrary", "arbitrary")),
        name="combine",
    )(starts_flat, x2d, mods, posq, yg)


def _pool_kernel(x_ref, ssq_ref, g_ref, sh_ref, sc_ref, o_ref, hs_ref, *, N, D, windows, chunks_per_group):
    b = pl.program_id(0)
    j = pl.program_id(1)
    P = max(windows) // 2
    cw = x_ref.shape[2]
    nblk = N // TOK_TILE
    g = g_ref[0]
    sh = sh_ref[0, pl.ds(b, 1), :]
    sc = sc_ref[0, pl.ds(b, 1), :]
    zeros = jnp.zeros((P, cw), F32)
    hs_ref[0:P, :] = zeros
    hs_ref[P + N:P + N + P, :] = zeros

    def fill(k, carry):
        r0 = pl.multiple_of(k * TOK_TILE, TOK_TILE)
        rstd = lax.rsqrt(ssq_ref[pl.ds(r0, TOK_TILE), :] * (1.0 / D) + NORM_EPS)
        y = x_ref[0, pl.ds(r0, TOK_TILE), :] * rstd * g
        hs_ref[pl.ds(pl.multiple_of(P + r0, 8), TOK_TILE), :] = y * (1.0 + sc) + sh
        return carry

    lax.fori_loop(0, nblk, fill, 0)

    for grp, w in enumerate(windows):
        half = w // 2

        @pl.when(j // chunks_per_group == grp)
        def _(half=half):
            def pool(k, carry):
                r0 = pl.multiple_of(k * TOK_TILE, TOK_TILE)
                big = hs_ref[pl.ds(r0, TOK_TILE + 2 * P), :]
                nrow = TOK_TILE + 2 * P
                fwd = big
                span = 1
                while span < 2 * half:
                    fwd = fwd + pltpu.roll(fwd, nrow - span, 0)
                    span *= 2
                acc = fwd[P - half:P - half + TOK_TILE]
                tk = lax.broadcasted_iota(I32, (TOK_TILE, 1), 0) + r0
                cnt = (jnp.minimum(tk + half, N) - jnp.maximum(tk - half, 0)).astype(F32)
                o_ref[0, pl.ds(r0, TOK_TILE), :] = (acc / cnt - big[P:P + TOK_TILE]).astype(o_ref.dtype)
                return carry

            lax.fori_loop(0, nblk, pool, 0)


def pool_mixer_input(x3, ssq, g3, mods, layer, cfg, cw):
    B, N, D = x3.shape
    dg = D // cfg.G
    cw = min(cw, dg)
    P = max(cfg.windows) // 2
    nj = D // cw

    def mspec(chunk):
        return pl.BlockSpec((1, 8, cw), lambda b, j: (layer, 0, chunk * nj + j))

    return pl.pallas_call(
        functools.partial(_pool_kernel, N=N, D=D, windows=cfg.windows, chunks_per_group=dg // cw),
        grid=(B, nj),
        in_specs=[pl.BlockSpec((1, N, cw), lambda b, j: (b, 0, j)),
                  pl.BlockSpec((N, 1), lambda b, j: (b, 0)),
                  pl.BlockSpec((1, 1, cw), lambda b, j: (layer, 0, j)),
                  mspec(0), mspec(1)],
        out_specs=pl.BlockSpec((1, N, cw), lambda b, j: (b, 0, j)),
        out_shape=jax.ShapeDtypeStruct((B, N, D), BF16),
        scratch_shapes=[pltpu.VMEM((N + 2 * P, cw), F32)],
        compiler_params=_cp(("parallel", "parallel")),
        name="pool",
    )(x3, ssq, g3, mods, mods)


def moe_block(x2d, mods, layer, g_ffn3, w_router, w_gate, w_up, w_down, cfg):
    B, N, D, E, cap = cfg.B, cfg.N, cfg.D, cfg.E, cfg.CAP
    nt = N // TOK_TILE
    wr_pad = jnp.pad(w_router, ((0, 0), (0, LANE - E)))
    hfx = ffn_norm_router(x2d.reshape(B, N, D), g_ffn3, mods, layer, wr_pad, E, tm=512)
    idx, posq, starts = route(hfx, cfg)
    idx_flat = jnp.transpose(idx, (1, 0, 2)).reshape(-1)
    starts_flat = jnp.transpose(starts[:, :, :nt + 1], (0, 2, 1)).reshape(-1)
    xs, gaff = gather_rows(idx_flat, hfx.reshape(B * N, D // 2 + LANE), R=min(256, cap))
    yg = experts(xs, gaff, w_gate, w_up, w_down, layer, tf=512, tn=512)
    return combine(starts_flat, x2d, mods, layer, posq, yg, cfg)


def forward(cfg, x, c, ctx, c_ctx, ada_w, ada_b, norm_mix_g, norm_ffn_g, na_w_qkv, na_q_gain,
            na_k_gain, na_rpb, na_w_out, pool_w, pool_scale, moe_w_router, moe_w_gate,
            moe_w_up, moe_w_down):
    B, N, D, NC, H = cfg.B, cfg.N, cfg.D, cfg.NC, cfg.H
    dh = D // H
    L = ada_w.shape[0]
    assert L == 2 and B + 1 <= 8
    cond8 = jnp.concatenate([c, c_ctx[None, :], jnp.zeros((8 - B - 1, D), F32)], axis=0)
    mods = ada_mods(cond8, ada_w, ada_b)
    gmix3 = norm_mix_g.reshape(L, 1, D)
    gffn3 = norm_ffn_g.reshape(L, 1, D)

    h = normmod(x, gmix3, mods, 0, 0, None, tm=512).reshape(B * N, D)
    hc = normmod(ctx, gmix3, mods, 0, 0, B, tm=512).reshape(B * NC, D)
    scale = 1.0 / math.sqrt(dh)
    gain = jnp.concatenate([jnp.tile(na_q_gain[0] * scale, H), jnp.tile(na_k_gain[0], H),
                            jnp.ones((D,), F32)])[None, :]
    w_qkv = na_w_qkv[0]
    qk = qkv_proj(h, w_qkv, gain, 0, 2 * D, dh, True, tm=2048, tn=256)
    v = qkv_proj(h, w_qkv, gain, 2 * D, D, dh, False, tm=2048, tn=256)
    kc = qkv_proj(hc, w_qkv, gain, D, D, dh, True, tm=1024, tn=256)
    vc = qkv_proj(hc, w_qkv, gain, 2 * D, D, dh, False, tm=1024, tn=256)
    rpb_pad = jnp.pad(na_rpb[0], ((0, 0), (0, 1), (0, LANE - (2 * NA_KW - 1))))
    o = attention(qk, v, kc, vc, rpb_pad, cfg)
    x1 = proj_residual(o, na_w_out[0][None], x.reshape(B * N, D), mods, 0, 2, N, None, tm=2048, tn=256)
    x2, ssq = moe_block(x1, mods, 0, gffn3, moe_w_router[0], moe_w_gate, moe_w_up, moe_w_down, cfg)

    pooled = pool_mixer_input(x2.reshape(B, N, D), ssq, gmix3, mods, 1, cfg, cw=256).reshape(B * N, D)
    x3 = proj_residual(pooled, pool_w[0], x2, mods, 1, 2, N, pool_scale[0][None, :], tm=1024, tn=1024)
    x4, _ = moe_block(x3, mods, 1, gffn3, moe_w_router[1], moe_w_gate, moe_w_up, moe_w_down, cfg)
    return x4.reshape(B, N, D)


def kernel(x, c, ctx, c_ctx, ada_w, ada_b, norm_mix_g, norm_ffn_g, na_w_qkv, na_q_gain, na_k_gain, na_rpb, na_w_out, pool_w, pool_scale, moe_w_router, moe_w_gate, moe_w_up, moe_w_down):
    B, N, D = x.shape
    E, _, F = moe_w_gate.shape[1:]
    G = pool_w.shape[1]
    cfg = Cfg(B=B, N=N, D=D, NC=ctx.shape[1], H=na_rpb.shape[1], GW=64, E=E,
              CAP=max(1, 2 * N // E), F=F, G=G, windows=(2, 4, 8, 16), n_ada=6)
    return forward(cfg, x, c, ctx, c_ctx, ada_w, ada_b, norm_mix_g, norm_ffn_g, na_w_qkv, na_q_gain,
                   na_k_gain, na_rpb, na_w_out, pool_w, pool_scale, moe_w_router, moe_w_gate,
                   moe_w_up, moe_w_down)
```

```python
import functools
import math
from typing import NamedTuple

import jax
import jax.numpy as jnp
from jax import lax
from jax.experimental import pallas as pl
from jax.experimental.pallas import tpu as pltpu

F32 = jnp.float32
BF16 = jnp.bfloat16
I32 = jnp.int32
U32 = jnp.uint32

LANE = 128
NORM_EPS = 1e-6
NEG_INF = -1e30
NA_KH = 8
NA_KW = 16
ATTN_QROWS = 4
ATTN_KROWS = ATTN_QROWS + NA_KH
TOK_TILE = 256
COMB_WIN = 64
BF16_ROWS = 16
MM_CHUNK = 512
EXPERT_KSPLIT = 2
VMEM_LIMIT = 56 * 1024 * 1024


class Cfg(NamedTuple):
    B: int
    N: int
    D: int
    NC: int
    H: int
    GW: int
    E: int
    CAP: int
    F: int
    G: int
    windows: tuple
    n_ada: int


def _cp(sem, vmem=VMEM_LIMIT):
    return pltpu.CompilerParams(dimension_semantics=sem, vmem_limit_bytes=vmem)


def _dot(a, b):
    return jnp.dot(a, b, preferred_element_type=F32)


def _dot_nt(a, b):
    return lax.dot_general(a, b, (((1,), (1,)), ((), ())), preferred_element_type=F32)


def _silu(x):
    return x * (1.0 / (1.0 + jnp.exp(-x)))


def _row_chunks(tm):
    mc = min(MM_CHUNK, tm)
    return [(r, mc) for r in range(0, tm, mc)]


def _ada_kernel(cond_ref, w_ref, b_ref, o_ref):
    s = _silu(cond_ref[...]).astype(BF16)
    o_ref[0] = _dot(s, w_ref[0].astype(BF16)) + b_ref[0]


def ada_mods(cond8, ada_w, ada_b):
    L, D, ND = ada_w.shape
    tn = min(512, ND)
    return pl.pallas_call(
        _ada_kernel,
        grid=(L, ND // tn),
        in_specs=[pl.BlockSpec((8, D), lambda l, j: (0, 0)),
                  pl.BlockSpec((1, D, tn), lambda l, j: (l, 0, j)),
                  pl.BlockSpec((1, 1, tn), lambda l, j: (l, 0, j))],
        out_specs=pl.BlockSpec((1, 8, tn), lambda l, j: (l, 0, j)),
        out_shape=jax.ShapeDtypeStruct((L, 8, ND), F32),
        compiler_params=_cp(("parallel", "parallel")),
        name="ada",
    )(cond8, ada_w, ada_b.reshape(L, 1, ND))


def _norm_mod(x, g, sh, sc):
    ms = jnp.mean(x * x, axis=-1, keepdims=True)
    y = x * lax.rsqrt(ms + NORM_EPS) * g
    return y * (1.0 + sc) + sh


def _normmod_kernel(x_ref, g_ref, sh_ref, sc_ref, o_ref, *, row):
    r = pl.program_id(0) if row is None else row
    sh = sh_ref[0, pl.ds(r, 1), :]
    sc = sc_ref[0, pl.ds(r, 1), :]
    o_ref[0] = _norm_mod(x_ref[0], g_ref[0], sh, sc).astype(o_ref.dtype)


def _mod_spec(D, layer, chunk):
    return pl.BlockSpec((1, 8, D), lambda b, i: (layer, 0, chunk))


def normmod(x3, g3, mods, layer, sh_chunk, row, tm):
    B, n, D = x3.shape
    tm = min(tm, n)
    return pl.pallas_call(
        functools.partial(_normmod_kernel, row=row),
        grid=(B, n // tm),
        in_specs=[pl.BlockSpec((1, tm, D), lambda b, i: (b, i, 0)),
                  pl.BlockSpec((1, 1, D), lambda b, i: (layer, 0, 0)),
                  _mod_spec(D, layer, sh_chunk),
                  _mod_spec(D, layer, sh_chunk + 1)],
        out_specs=pl.BlockSpec((1, tm, D), lambda b, i: (b, i, 0)),
        out_shape=jax.ShapeDtypeStruct((B, n, D), BF16),
        compiler_params=_cp(("parallel", "parallel")),
        name="normmod",
    )(x3, g3, mods, mods)


def _qkv_kernel(a_ref, w_ref, gain_ref, o_ref, *, norm, dh):
    wb = w_ref[...].astype(BF16)
    tn = wb.shape[1]
    for r0, mc in _row_chunks(a_ref.shape[0]):
        acc = _dot(a_ref[r0:r0 + mc, :], wb)
        for c in range(tn // dh):
            y = acc[:, c * dh:(c + 1) * dh]
            if norm:
                ms = jnp.mean(y * y, axis=-1, keepdims=True)
                y = y * lax.rsqrt(ms + NORM_EPS) * gain_ref[:, c * dh:(c + 1) * dh]
            o_ref[c, r0:r0 + mc, :] = y.astype(o_ref.dtype)


def qkv_proj(a, w, gain, col0, ncols, dh, norm, tm, tn):
    M, D = a.shape
    tm = min(tm, M)
    off = col0 // tn
    return pl.pallas_call(
        functools.partial(_qkv_kernel, norm=norm, dh=dh),
        grid=(M // tm, ncols // tn),
        in_specs=[pl.BlockSpec((tm, D), lambda i, j: (i, 0)),
                  pl.BlockSpec((D, tn), lambda i, j: (0, j + off)),
                  pl.BlockSpec((1, tn), lambda i, j: (0, j + off))],
        out_specs=pl.BlockSpec((tn // dh, tm, dh), lambda i, j: (j, i, 0)),
        out_shape=jax.ShapeDtypeStruct((ncols // dh, M, dh), BF16),
        compiler_params=_cp(("parallel", "arbitrary")),
        name="qkv",
    )(a, w, gain)


def _attn_bias_variants():
    out = []
    for var in range(3):
        tab = {}
        for i in range(ATTN_QROWS):
            for j in range(ATTN_KROWS):
                if var == 0:
                    valid, dr = j < NA_KH, j - i + NA_KH - 1
                elif var == 1:
                    valid, dr = i <= j < i + NA_KH, j - i + NA_KH // 2 - 1
                else:
                    valid, dr = j >= ATTN_KROWS - NA_KH, j - i - 1
                tab[(i, j)] = (valid, dr)
        out.append(tab)
    return out


def _attn_kernel(rpb_ref, q_ref, k_ref, v_ref, kc_ref, vc_ref, o_ref, bias_ref, sl_ref, sc_ref,
                 pl_ref, pc_ref, den_ref, *, gw, rows):
    nblk = rows // ATTN_QROWS
    qn = ATTN_QROWS * gw
    kn = ATTN_KROWS * gw

    @pl.when(pl.program_id(1) == 0)
    def _build_bias():
        qc = lax.broadcasted_iota(I32, (gw, gw), 0)
        kc = lax.broadcasted_iota(I32, (gw, gw), 1)
        cs = jnp.clip(qc - NA_KW // 2, 0, gw - NA_KW)
        col_mask = jnp.where((kc >= cs) & (kc < cs + NA_KW), 0.0, NEG_INF).astype(F32)
        tiles = []
        for dr in range(2 * NA_KH - 1):
            r = jnp.broadcast_to(rpb_ref[0, dr:dr + 1, :], (gw, LANE))
            t = pltpu.roll(r, LANE - (NA_KW - 1), 1, stride=1, stride_axis=0)
            tiles.append(t[:, :gw] + col_mask)
        neg = jnp.full((gw, gw), NEG_INF, F32)
        for var, tab in enumerate(_attn_bias_variants()):
            for (i, j), (valid, dr) in tab.items():
                bias_ref[var, i * gw:(i + 1) * gw, j * gw:(j + 1) * gw] = tiles[dr] if valid else neg

    def geom(blk):
        if isinstance(blk, int):
            r0 = blk * ATTN_QROWS
            start = min(max(r0 - NA_KH // 2, 0), rows - ATTN_KROWS)
            var = 0 if blk == 0 else (2 if blk == nblk - 1 else 1)
            return r0 * gw, start * gw, var
        r0 = blk * ATTN_QROWS
        start = jnp.clip(r0 - NA_KH // 2, 0, rows - ATTN_KROWS)
        var = jnp.where(blk == 0, 0, jnp.where(blk == nblk - 1, 2, 1))
        return pl.multiple_of(r0 * gw, qn), pl.multiple_of(start * gw, gw), var

    def scores(blk, slot):
        q0, k0, var = geom(blk)
        q = q_ref[pl.ds(q0, qn), :]
        sl_ref[slot] = _dot_nt(q, k_ref[pl.ds(k0, kn), :]) + bias_ref[var]
        sc_ref[slot] = _dot_nt(q, kc_ref[...])

    def softmax(slot):
        s_lat = sl_ref[slot]
        s_ctx = sc_ref[slot]
        m = jnp.maximum(jnp.max(s_lat, axis=-1, keepdims=True), jnp.max(s_ctx, axis=-1, keepdims=True))
        p_lat = jnp.exp(s_lat - m)
        p_ctx = jnp.exp(s_ctx - m)
        den_ref[slot] = jnp.sum(p_lat, axis=-1, keepdims=True) + jnp.sum(p_ctx, axis=-1, keepdims=True)
        pl_ref[slot] = p_lat.astype(BF16)
        pc_ref[slot] = p_ctx.astype(BF16)

    def values(blk, slot):
        q0, k0, _ = geom(blk)
        o = _dot(pl_ref[slot], v_ref[pl.ds(k0, kn), :]) + _dot(pc_ref[slot], vc_ref[...])
        o_ref[pl.ds(q0, qn), :] = (o * (1.0 / den_ref[slot])).astype(o_ref.dtype)

    scores(0, 0)
    scores(1, 1)
    softmax(0)

    def body(it, carry):
        i = 2 + 2 * it
        scores(i, 0)
        softmax(1)
        values(i - 2, 0)
        scores(i + 1, 1)
        softmax(0)
        values(i - 1, 1)
        return carry

    lax.fori_loop(0, (nblk - 2) // 2, body, 0)
    softmax(1)
    values(nblk - 2, 0)
    values(nblk - 1, 1)


def attention(qk, v, kc, vc, rpb_pad, cfg):
    B, N, D, NC, H, GW = cfg.B, cfg.N, cfg.D, cfg.NC, cfg.H, cfg.GW
    dh = D // H
    rows = N // GW
    assert dh == LANE and rows >= ATTN_KROWS and rows % (2 * ATTN_QROWS) == 0 and GW >= NA_KW
    return pl.pallas_call(
        functools.partial(_attn_kernel, gw=GW, rows=rows),
        grid=(H, B),
        in_specs=[pl.BlockSpec((1, 2 * NA_KH, LANE), lambda h, b: (h, 0, 0)),
                  pl.BlockSpec((None, N, dh), lambda h, b: (h, b, 0)),
                  pl.BlockSpec((None, N, dh), lambda h, b: (H + h, b, 0)),
                  pl.BlockSpec((None, N, dh), lambda h, b: (h, b, 0)),
                  pl.BlockSpec((None, NC, dh), lambda h, b: (h, b, 0)),
                  pl.BlockSpec((None, NC, dh), lambda h, b: (h, b, 0))],
        out_specs=pl.BlockSpec((N, dh), lambda h, b: (b, h)),
        out_shape=jax.ShapeDtypeStruct((B * N, D), BF16),
        scratch_shapes=[pltpu.VMEM((3, ATTN_QROWS * GW, ATTN_KROWS * GW), F32),
                        pltpu.VMEM((2, ATTN_QROWS * GW, ATTN_KROWS * GW), F32),
                        pltpu.VMEM((2, ATTN_QROWS * GW, NC), F32),
                        pltpu.VMEM((2, ATTN_QROWS * GW, ATTN_KROWS * GW), BF16),
                        pltpu.VMEM((2, ATTN_QROWS * GW, NC), BF16),
                        pltpu.VMEM((2, ATTN_QROWS * GW, 1), F32)],
        compiler_params=_cp(("arbitrary", "arbitrary")),
        name="attn",
    )(rpb_pad, qk, qk, v, kc, vc)


def _proj_res_kernel(a_ref, w_ref, x_ref, g_ref, *rest, tiles_per_batch, scaled):
    if scaled:
        ps_ref, o_ref = rest
    else:
        (o_ref,) = rest
    wb = w_ref[0].astype(BF16)
    b = pl.program_id(0) // tiles_per_batch
    g = g_ref[0, pl.ds(b, 1), :]
    for r0, mc in _row_chunks(a_ref.shape[0]):
        y = _dot(a_ref[r0:r0 + mc, :], wb)
        if scaled:
            y = y * ps_ref[...]
        o_ref[r0:r0 + mc, :] = x_ref[r0:r0 + mc, :] + g * y


def proj_residual(a, w3, x2, mods, layer, gate_chunk, n_per_batch, pscale, tm, tn):
    M = a.shape[0]
    G, K, KO = w3.shape
    Dout = G * KO
    tm = min(tm, n_per_batch)
    tn = min(tn, KO)
    nj = KO // tn
    scaled = pscale is not None
    in_specs = [pl.BlockSpec((tm, K), lambda i, g, j: (i, g)),
                pl.BlockSpec((1, K, tn), lambda i, g, j: (g, 0, j)),
                pl.BlockSpec((tm, tn), lambda i, g, j: (i, g * nj + j)),
                pl.BlockSpec((1, 8, tn), lambda i, g, j: (layer, 0, gate_chunk * (Dout // tn) + g * nj + j))]
    args = [a, w3, x2, mods]
    if scaled:
        in_specs.append(pl.BlockSpec((1, tn), lambda i, g, j: (0, g * nj + j)))
        args.append(pscale)
    return pl.pallas_call(
        functools.partial(_proj_res_kernel, tiles_per_batch=n_per_batch // tm, scaled=scaled),
        grid=(M // tm, G, nj),
        in_specs=in_specs,
        out_specs=pl.BlockSpec((tm, tn), lambda i, g, j: (i, g * nj + j)),
        out_shape=jax.ShapeDtypeStruct((M, Dout), F32),
        compiler_params=_cp(("parallel", "arbitrary", "arbitrary")),
        name="proj_res",
    )(*args)


def _split_bf16(x):
    hi = x.astype(BF16)
    lo = (x - hi.astype(F32)).astype(BF16)
    return hi, lo


def _ffn_norm_kernel(x_ref, g_ref, sh_ref, sc_ref, wr_ref, hfx_ref, *, E):
    b = pl.program_id(0)
    D = x_ref.shape[2]
    hf = _norm_mod(x_ref[0], g_ref[0], sh_ref[0, pl.ds(b, 1), :], sc_ref[0, pl.ds(b, 1), :])
    bits = pltpu.bitcast(hf.astype(BF16).astype(F32), U32)
    hfx_ref[0, :, 0:D // 2] = (bits[:, 0:D // 2] >> 16) | bits[:, D // 2:D]
    h_hi, h_lo = _split_bf16(hf)
    w_hi, w_lo = _split_bf16(wr_ref[...])
    lg = _dot(h_hi, w_hi) + (_dot(h_hi, w_lo) + _dot(h_lo, w_hi))
    lane = lax.broadcasted_iota(I32, (1, LANE), 1)
    lg = jnp.where(lane < E, lg, NEG_INF)
    ex = jnp.exp(lg - jnp.max(lg, axis=-1, keepdims=True))
    aff = ex / jnp.sum(ex, axis=-1, keepdims=True)
    hfx_ref[0, :, D // 2:D // 2 + LANE] = pltpu.bitcast(aff, U32)


def ffn_norm_router(x3, g3, mods, layer, wr_pad, E, tm):
    B, N, D = x3.shape
    tm = min(tm, N)
    return pl.pallas_call(
        functools.partial(_ffn_norm_kernel, E=E),
        grid=(B, N // tm),
        in_specs=[pl.BlockSpec((1, tm, D), lambda b, i: (b, i, 0)),
                  pl.BlockSpec((1, 1, D), lambda b, i: (layer, 0, 0)),
                  _mod_spec(D, layer, 3),
                  _mod_spec(D, layer, 4),
                  pl.BlockSpec((D, LANE), lambda b, i: (0, 0))],
        out_specs=pl.BlockSpec((1, tm, D // 2 + LANE), lambda b, i: (b, i, 0)),
        out_shape=jax.ShapeDtypeStruct((B, N, D // 2 + LANE), U32),
        compiler_params=_cp(("parallel", "parallel")),
        name="ffn_norm",
    )(x3, g3, mods, mods, wr_pad)


def _route_kernel(aff_ref, idx_ref, posq_ref, starts_ref, t_ref, cum_ref, *, E, cap, N):
    b = pl.program_id(0)
    nblk = N // TOK_TILE
    logn = int(math.log2(N))
    for k in range(N // LANE):
        t_ref[:, k * LANE:(k + 1) * LANE] = pltpu.bitcast(aff_ref[0, k * LANE:(k + 1) * LANE, :], F32).T
    bits = pltpu.bitcast(t_ref[0:E, :], I32)
    tok = lax.broadcasted_iota(I32, (1, N), 1)

    def count(pred):
        return jnp.sum(jnp.where(pred, 1, 0), axis=1, keepdims=True)

    def thr_step(i, thr):
        cand = thr | (jnp.int32(1) << (30 - i))
        return jnp.where(count(bits >= cand) >= cap, cand, thr)

    thr = lax.fori_loop(0, 31, thr_step, jnp.zeros((E, 1), I32))
    gt = bits > thr
    eq = bits == thr
    need = cap - count(gt)

    def tie_step(i, ans):
        cand = ans | (jnp.int32(1) << (logn - 1 - i))
        return jnp.where(count(eq & (tok < cand)) < need, cand, ans)

    last = lax.fori_loop(0, logn, tie_step, jnp.zeros((E, 1), I32))
    sel = gt | (eq & (tok <= last))
    sel_f = jnp.where(sel, 1.0, 0.0).astype(F32)

    ri = lax.broadcasted_iota(I32, (TOK_TILE, TOK_TILE), 0)
    ci = lax.broadcasted_iota(I32, (TOK_TILE, TOK_TILE), 1)
    tri = jnp.where(ri <= ci, 1.0, 0.0).astype(BF16)
    lane_s = lax.broadcasted_iota(I32, (1, LANE), 1)
    carry = jnp.zeros((E, 1), F32)
    starts = jnp.zeros((E, LANE), F32)
    for k in range(nblk):
        m = sel_f[:, k * TOK_TILE:(k + 1) * TOK_TILE]
        starts = jnp.where(lane_s == k, carry, starts)
        cum_ref[:, k * TOK_TILE:(k + 1) * TOK_TILE] = _dot(m.astype(BF16), tri) + carry
        carry = carry + jnp.sum(m, axis=1, keepdims=True)
    starts_ref[0] = jnp.where(lane_s == nblk, carry, starts).astype(I32)

    lane_c = lax.broadcasted_iota(I32, (1, cap), 1)

    def slot_step(c, out):
        cnt = jnp.sum(jnp.where(cum_ref[...] <= lax.convert_element_type(c, F32), 1, 0), axis=1, keepdims=True)
        return jnp.where(lane_c == c, cnt, out)

    idx = lax.fori_loop(0, cap, slot_step, jnp.zeros((E, cap), I32), unroll=4)
    idx_ref[0] = idx + b * N

    t_ref[0:E, :] = jnp.where(sel, cum_ref[...], 0.0)
    t_ref[E:LANE, :] = jnp.zeros((LANE - E, N), F32)
    for k in range(N // LANE):
        posq_ref[0, k * LANE:(k + 1) * LANE, :] = t_ref[:, k * LANE:(k + 1) * LANE].T


def route(hfx, cfg):
    B, N, D, E, cap = cfg.B, cfg.N, cfg.D, cfg.E, cfg.CAP
    nblk = N // TOK_TILE
    assert N % TOK_TILE == 0 and cap % LANE == 0 and (1 << int(math.log2(N))) == N
    assert E % 8 == 0 and nblk < LANE
    return pl.pallas_call(
        functools.partial(_route_kernel, E=E, cap=cap, N=N),
        grid=(B,),
        in_specs=[pl.BlockSpec((1, N, LANE), lambda b: (b, 0, D // 2 // LANE))],
        out_specs=[pl.BlockSpec((1, E, cap), lambda b: (b, 0, 0)),
                   pl.BlockSpec((1, N, LANE), lambda b: (b, 0, 0)),
                   pl.BlockSpec((1, E, LANE), lambda b: (b, 0, 0))],
        out_shape=[jax.ShapeDtypeStruct((B, E, cap), I32),
                   jax.ShapeDtypeStruct((B, N, LANE), F32),
                   jax.ShapeDtypeStruct((B, E, LANE), I32)],
        scratch_shapes=[pltpu.VMEM((LANE, N), F32), pltpu.VMEM((E, N), F32)],
        compiler_params=_cp(("arbitrary",)),
        name="route",
    )(hfx)


def _expert_kernel(idx_ref, hfx_hbm, wg_ref, wu_ref, wd_ref, o_ref, xs_ref, ga_ref, stg_ref, hid_ref,
                   ag_ref, au_ref, sem, *, nf, nk, tf, nn):
    e = pl.program_id(0)
    s = pl.program_id(1)
    ne = pl.num_programs(0)
    rows, D = xs_ref.shape
    dk = D // nk
    rpt = rows // nn

    def row_copy(expert, r):
        tok = idx_ref[expert * rows + r]
        return pltpu.make_async_copy(hfx_hbm.at[pl.ds(tok, 1), :], stg_ref.at[pl.ds(r, 1), :], sem.at[0])

    @pl.when((e == 0) & (s == 0))
    def _():
        def body(r, carry):
            row_copy(0, r).start()
            return carry
        lax.fori_loop(0, rows, body, 0, unroll=8)

    @pl.when(s == 0)
    def _():
        pltpu.make_async_copy(hfx_hbm.at[pl.ds(0, rows), :], stg_ref, sem.at[0]).wait()
        for r0, mc in _row_chunks(rows):
            w = stg_ref[r0:r0 + mc, 0:D // 2]
            xs_ref[r0:r0 + mc, 0:D // 2] = pltpu.bitcast(w << 16, F32).astype(BF16)
            xs_ref[r0:r0 + mc, D // 2:D] = pltpu.bitcast(w & jnp.uint32(0xFFFF0000), F32).astype(BF16)
        ga_ref[...] = pltpu.bitcast(stg_ref[:, D // 2:D // 2 + LANE], F32)

    for kh in range(nk):
        @pl.when((s < nf * nk) & (s % nk == kh))
        def _(kh=kh):
            wg = wg_ref[0].astype(BF16)
            wu = wu_ref[0].astype(BF16)
            for r0, mc in _row_chunks(rows):
                x = xs_ref[r0:r0 + mc, kh * dk:(kh + 1) * dk]
                a = _dot(x, wg)
                u = _dot(x, wu)
                if kh > 0:
                    a = a + ag_ref[r0:r0 + mc, :]
                    u = u + au_ref[r0:r0 + mc, :]
                if kh < nk - 1:
                    ag_ref[r0:r0 + mc, :] = a
                    au_ref[r0:r0 + mc, :] = u
                else:
                    hid_ref[s // nk, r0:r0 + mc, :] = (_silu(a) * u).astype(BF16)

    @pl.when(s >= nf * nk)
    def _():
        nxt = jnp.where(e + 1 < ne, e + 1, 0)
        base = (s - nf * nk) * rpt
        for j in range(rpt):
            row_copy(nxt, base + j).start()
        wd = wd_ref[0].astype(BF16)
        lane = lax.broadcasted_iota(I32, (1, LANE), 1)
        for r0, mc in _row_chunks(rows):
            acc = _dot(hid_ref[0, r0:r0 + mc, :], wd[0:tf, :])
            for k in range(1, nf):
                acc = acc + _dot(hid_ref[k, r0:r0 + mc, :], wd[k * tf:(k + 1) * tf, :])
            gate = jnp.sum(jnp.where(lane == e, ga_ref[r0:r0 + mc, :], 0.0), axis=-1, keepdims=True)
            o_ref[r0:r0 + mc, :] = (acc * gate).astype(o_ref.dtype)

    @pl.when((e == ne - 1) & (s == pl.num_programs(1) - 1))
    def _():
        pltpu.make_async_copy(hfx_hbm.at[pl.ds(0, rows), :], stg_ref, sem.at[0]).wait()


def experts(idx_flat, hfx2, w_gate, w_up, w_down, layer, rows, tf, tn):
    _, E, D, F = w_gate.shape
    M = idx_flat.shape[0]
    DX = hfx2.shape[1]
    assert M == E * rows and DX == D // 2 + LANE
    tf = min(tf, F)
    tn = min(tn, D)
    nf, nn = F // tf, D // tn
    nk = EXPERT_KSPLIT
    n1 = nf * nk
    assert rows % nn == 0

    def w_in(e, s, idx):
        s1 = jnp.minimum(s, n1 - 1)
        return (layer, e, s1 % nk, s1 // nk)

    return pl.pallas_call(
        functools.partial(_expert_kernel, nf=nf, nk=nk, tf=tf, nn=nn),
        grid_spec=pltpu.PrefetchScalarGridSpec(
            num_scalar_prefetch=1,
            grid=(E, n1 + nn),
            in_specs=[pl.BlockSpec(memory_space=pl.ANY),
                      pl.BlockSpec((None, 1, D // nk, tf), w_in),
                      pl.BlockSpec((None, 1, D // nk, tf), w_in),
                      pl.BlockSpec((None, 1, F, tn), lambda e, s, idx: (layer, e, 0, jnp.maximum(s - n1, 0)))],
            out_specs=pl.BlockSpec((rows, tn), lambda e, s, idx: (e, jnp.maximum(s - n1, 0))),
            scratch_shapes=[pltpu.VMEM((rows, D), BF16),
                            pltpu.VMEM((rows, LANE), F32),
                            pltpu.VMEM((rows, DX), U32),
                            pltpu.VMEM((nf, rows, tf), BF16),
                            pltpu.VMEM((rows, tf), F32), pltpu.VMEM((rows, tf), F32),
                            pltpu.SemaphoreType.DMA((1,))]),
        out_shape=jax.ShapeDtypeStruct((M, D), BF16),
        compiler_params=_cp(("arbitrary", "arbitrary")),
        name="experts",
    )(idx_flat, hfx2, w_gate, w_up, w_down)


def _combine_kernel(st_ref, x_ref, g_ref, posq_ref, yg_hbm, o_ref, ssq_ref, ybuf, acc_ref, sem,
                    *, E, cap, B, nt):
    b = pl.program_id(0)
    t = pl.program_id(1)
    W = COMB_WIN
    KW = E * W
    step = b * nt + t
    slot = step % 2

    def tile_lo(bb, tt):
        base = (bb * (nt + 1) + tt) * E
        return [(st_ref[base + e] // BF16_ROWS) * BF16_ROWS for e in range(E)]

    def window_copy(bb, e, start, sl):
        row0 = pl.multiple_of((e * B + bb) * cap + start, BF16_ROWS)
        return pltpu.make_async_copy(yg_hbm.at[pl.ds(row0, W), :], ybuf.at[sl, pl.ds(e * W, W), :], sem.at[sl])

    def start_round0(bb, tt, sl):
        lo_n = tile_lo(bb, tt)
        for e in range(E):
            window_copy(bb, e, jnp.minimum(lo_n[e], cap - W), sl).start()

    @pl.when(step == 0)
    def _():
        start_round0(b, t, slot)

    @pl.when(step + 1 < B * nt)
    def _():
        start_round0((step + 1) // nt, (step + 1) % nt, 1 - slot)

    lo = tile_lo(b, t)
    base = (b * (nt + 1) + t) * E
    hi = [st_ref[base + E + e] for e in range(E)]
    nrounds = jnp.int32(1)
    for e in range(E):
        nrounds = jnp.maximum(nrounds, (hi[e] - lo[e] + (W - 1)) // W)

    qi = posq_ref[0].astype(I32)
    erow = lax.broadcasted_iota(I32, (LANE, KW), 0)
    ecol = lax.broadcasted_iota(I32, (LANE, KW), 1) // W
    expand = jnp.where(erow == ecol, 1.0, 0.0).astype(BF16)
    qe = (_dot((qi >> 5).astype(F32).astype(BF16), expand) * 32.0
          + _dot((qi & 31).astype(F32).astype(BF16), expand)).astype(I32) - 1
    lane_e = lax.broadcasted_iota(I32, (1, KW), 1) // W
    lane_k = lax.broadcasted_iota(I32, (1, KW), 1) % W

    def one_round(r, prefetched):
        lo_r = [lo[e] + r * W for e in range(E)]
        st = [jnp.minimum(lo_r[e], cap - W) for e in range(E)]
        if not prefetched:
            for e in range(E):
                window_copy(b, e, st[e], slot).start()
        stv = jnp.zeros((1, KW), I32)
        lov = jnp.zeros((1, KW), I32)
        for e in range(E):
            stv = jnp.where(lane_e == e, st[e], stv)
            lov = jnp.where(lane_e == e, lo_r[e], lov)
        onehot = (qe - stv == lane_k) & (qe >= lov) & (qe < lov + W)
        s = jnp.where(onehot, 1.0, 0.0).astype(BF16)
        for e in range(E):
            window_copy(b, e, st[e], slot).wait()
        return _dot(s, ybuf[slot])

    acc_ref[...] = one_round(0, True)

    def extra(r, carry):
        acc_ref[...] += one_round(r, False)
        return carry

    lax.fori_loop(1, nrounds, extra, 0)
    x2 = x_ref[...] + g_ref[0, pl.ds(b, 1), :] * acc_ref[...]
    o_ref[...] = x2
    ssq_ref[...] = jnp.sum(x2 * x2, axis=-1, keepdims=True)


def combine(starts_flat, x2d, mods, layer, posq, yg, cfg):
    B, N, D, E, cap = cfg.B, cfg.N, cfg.D, cfg.E, cfg.CAP
    nt = N // TOK_TILE
    assert cap >= COMB_WIN and cap % COMB_WIN == 0 and COMB_WIN % BF16_ROWS == 0 and cap <= 32 * 32
    return pl.pallas_call(
        functools.partial(_combine_kernel, E=E, cap=cap, B=B, nt=nt),
        grid_spec=pltpu.PrefetchScalarGridSpec(
            num_scalar_prefetch=1,
            grid=(B, nt),
            in_specs=[pl.BlockSpec((TOK_TILE, D), lambda b, t, st: (b * nt + t, 0)),
                      pl.BlockSpec((1, 8, D), lambda b, t, st: (layer, 0, 5)),
                      pl.BlockSpec((1, TOK_TILE, LANE), lambda b, t, st: (b, t, 0)),
                      pl.BlockSpec(memory_space=pl.ANY)],
            out_specs=[pl.BlockSpec((TOK_TILE, D), lambda b, t, st: (b * nt + t, 0)),
                       pl.BlockSpec((TOK_TILE, 1), lambda b, t, st: (b * nt + t, 0))],
            scratch_shapes=[pltpu.VMEM((2, E * COMB_WIN, D), BF16),
                            pltpu.VMEM((TOK_TILE, D), F32),
                            pltpu.SemaphoreType.DMA((2,))]),
        out_shape=[jax.ShapeDtypeStruct((B * N, D), F32),
                   jax.ShapeDtypeStruct((B * N, 1), F32)],
        compiler_params=_cp(("arbitrary", "arbitrary")),
        name="combine",
    )(starts_flat, x2d, mods, posq, yg)


def _pool_kernel(x_ref, ssq_ref, g_ref, sh_ref, sc_ref, o_ref, hs_ref, *, N, D, windows, chunks_per_group):
    b = pl.program_id(0)
    j = pl.program_id(1)
    P = max(windows) // 2
    cw = x_ref.shape[2]
    nblk = N // TOK_TILE
    g = g_ref[0]
    sh = sh_ref[0, pl.ds(b, 1), :]
    sc = sc_ref[0, pl.ds(b, 1), :]
    zeros = jnp.zeros((P, cw), F32)
    hs_ref[0:P, :] = zeros
    hs_ref[P + N:P + N + P, :] = zeros

    def fill(k, carry):
        r0 = pl.multiple_of(k * TOK_TILE, TOK_TILE)
        rstd = lax.rsqrt(ssq_ref[pl.ds(r0, TOK_TILE), :] * (1.0 / D) + NORM_EPS)
        y = x_ref[0, pl.ds(r0, TOK_TILE), :] * rstd * g
        hs_ref[pl.ds(pl.multiple_of(P + r0, 8), TOK_TILE), :] = y * (1.0 + sc) + sh
        return carry

    lax.fori_loop(0, nblk, fill, 0)

    for grp, w in enumerate(windows):
        half = w // 2

        @pl.when(j // chunks_per_group == grp)
        def _(half=half):
            def pool(k, carry):
                r0 = pl.multiple_of(k * TOK_TILE, TOK_TILE)
                big = hs_ref[pl.ds(r0, TOK_TILE + 2 * P), :]
                nrow = TOK_TILE + 2 * P
                fwd = big
                span = 1
                while span < 2 * half:
                    fwd = fwd + pltpu.roll(fwd, nrow - span, 0)
                    span *= 2
                acc = fwd[P - half:P - half + TOK_TILE]
                tk = lax.broadcasted_iota(I32, (TOK_TILE, 1), 0) + r0
                cnt = (jnp.minimum(tk + half, N) - jnp.maximum(tk - half, 0)).astype(F32)
                o_ref[0, pl.ds(r0, TOK_TILE), :] = (acc / cnt - big[P:P + TOK_TILE]).astype(o_ref.dtype)
                return carry

            lax.fori_loop(0, nblk, pool, 0)


def pool_mixer_input(x3, ssq, g3, mods, layer, cfg, cw):
    B, N, D = x3.shape
    dg = D // cfg.G
    cw = min(cw, dg)
    P = max(cfg.windows) // 2
    nj = D // cw

    def mspec(chunk):
        return pl.BlockSpec((1, 8, cw), lambda b, j: (layer, 0, chunk * nj + j))

    return pl.pallas_call(
        functools.partial(_pool_kernel, N=N, D=D, windows=cfg.windows, chunks_per_group=dg // cw),
        grid=(B, nj),
        in_specs=[pl.BlockSpec((1, N, cw), lambda b, j: (b, 0, j)),
                  pl.BlockSpec((N, 1), lambda b, j: (b, 0)),
                  pl.BlockSpec((1, 1, cw), lambda b, j: (layer, 0, j)),
                  mspec(0), mspec(1)],
        out_specs=pl.BlockSpec((1, N, cw), lambda b, j: (b, 0, j)),
        out_shape=jax.ShapeDtypeStruct((B, N, D), BF16),
        scratch_shapes=[pltpu.VMEM((N + 2 * P, cw), F32)],
        compiler_params=_cp(("parallel", "parallel")),
        name="pool",
    )(x3, ssq, g3, mods, mods)


def moe_block(x2d, mods, layer, g_ffn3, w_router, w_gate, w_up, w_down, cfg):
    B, N, D, E, cap = cfg.B, cfg.N, cfg.D, cfg.E, cfg.CAP
    nt = N // TOK_TILE
    wr_pad = jnp.pad(w_router, ((0, 0), (0, LANE - E)))
    hfx = ffn_norm_router(x2d.reshape(B, N, D), g_ffn3, mods, layer, wr_pad, E, tm=512)
    idx, posq, starts = route(hfx, cfg)
    idx_flat = jnp.transpose(idx, (1, 0, 2)).reshape(-1)
    starts_flat = jnp.transpose(starts[:, :, :nt + 1], (0, 2, 1)).reshape(-1)
    yg = experts(idx_flat, hfx.reshape(B * N, D // 2 + LANE), w_gate, w_up, w_down, layer, B * cap,
                 tf=512, tn=512)
    return combine(starts_flat, x2d, mods, layer, posq, yg, cfg)


def forward(cfg, x, c, ctx, c_ctx, ada_w, ada_b, norm_mix_g, norm_ffn_g, na_w_qkv, na_q_gain,
            na_k_gain, na_rpb, na_w_out, pool_w, pool_scale, moe_w_router, moe_w_gate,
            moe_w_up, moe_w_down):
    B, N, D, NC, H = cfg.B, cfg.N, cfg.D, cfg.NC, cfg.H
    dh = D // H
    L = ada_w.shape[0]
    assert L == 2 and B + 1 <= 8
    cond8 = jnp.concatenate([c, c_ctx[None, :], jnp.zeros((8 - B - 1, D), F32)], axis=0)
    mods = ada_mods(cond8, ada_w, ada_b)
    gmix3 = norm_mix_g.reshape(L, 1, D)
    gffn3 = norm_ffn_g.reshape(L, 1, D)

    h = normmod(x, gmix3, mods, 0, 0, None, tm=512).reshape(B * N, D)
    hc = normmod(ctx, gmix3, mods, 0, 0, B, tm=512).reshape(B * NC, D)
    scale = 1.0 / math.sqrt(dh)
    gain = jnp.concatenate([jnp.tile(na_q_gain[0] * scale, H), jnp.tile(na_k_gain[0], H),
                            jnp.ones((D,), F32)])[None, :]
    w_qkv = na_w_qkv[0]
    qk = qkv_proj(h, w_qkv, gain, 0, 2 * D, dh, True, tm=2048, tn=256)
    v = qkv_proj(h, w_qkv, gain, 2 * D, D, dh, False, tm=2048, tn=256)
    kc = qkv_proj(hc, w_qkv, gain, D, D, dh, True, tm=1024, tn=256)
    vc = qkv_proj(hc, w_qkv, gain, 2 * D, D, dh, False, tm=1024, tn=256)
    rpb_pad = jnp.pad(na_rpb[0], ((0, 0), (0, 1), (0, LANE - (2 * NA_KW - 1))))
    o = attention(qk, v, kc, vc, rpb_pad, cfg)
    x1 = proj_residual(o, na_w_out[0][None], x.reshape(B * N, D), mods, 0, 2, N, None, tm=2048, tn=256)
    x2, ssq = moe_block(x1, mods, 0, gffn3, moe_w_router[0], moe_w_gate, moe_w_up, moe_w_down, cfg)

    pooled = pool_mixer_input(x2.reshape(B, N, D), ssq, gmix3, mods, 1, cfg, cw=256).reshape(B * N, D)
    x3 = proj_residual(pooled, pool_w[0], x2, mods, 1, 2, N, pool_scale[0][None, :], tm=2048, tn=1024)
    x4, _ = moe_block(x3, mods, 1, gffn3, moe_w_router[1], moe_w_gate, moe_w_up, moe_w_down, cfg)
    return x4.reshape(B, N, D)


def kernel(x, c, ctx, c_ctx, ada_w, ada_b, norm_mix_g, norm_ffn_g, na_w_qkv, na_q_gain, na_k_gain, na_rpb, na_w_out, pool_w, pool_scale, moe_w_router, moe_w_gate, moe_w_up, moe_w_down):
    B, N, D = x.shape
    E, _, F = moe_w_gate.shape[1:]
    G = pool_w.shape[1]
    cfg = Cfg(B=B, N=N, D=D, NC=ctx.shape[1], H=na_rpb.shape[1], GW=64, E=E,
              CAP=max(1, 2 * N // E), F=F, G=G, windows=(2, 4, 8, 16), n_ada=6)
    return forward(cfg, x, c, ctx, c_ctx, ada_w, ada_b, norm_mix_g, norm_ffn_g, na_w_qkv, na_q_gain,
                   na_k_gain, na_rpb, na_w_out, pool_w, pool_scale, moe_w_router, moe_w_gate,
                   moe_w_up, moe_w_down)
```

```python
import functools
import math
from typing import NamedTuple

import jax
import jax.numpy as jnp
from jax import lax
from jax.experimental import pallas as pl
from jax.experimental.pallas import tpu as pltpu

F32 = jnp.float32
BF16 = jnp.bfloat16
I32 = jnp.int32
U32 = jnp.uint32

LANE = 128
NORM_EPS = 1e-6
NEG_INF = -1e30
NA_KH = 8
NA_KW = 16
ATTN_QROWS = 4
ATTN_KROWS = ATTN_QROWS + NA_KH
TOK_TILE = 256
COMB_WIN = 64
BF16_ROWS = 16
MM_CHUNK = 512
EXPERT_KSPLIT = 2
VMEM_LIMIT = 56 * 1024 * 1024


class Cfg(NamedTuple):
    B: int
    N: int
    D: int
    NC: int
    H: int
    GW: int
    E: int
    CAP: int
    F: int
    G: int
    windows: tuple
    n_ada: int


def _cp(sem, vmem=VMEM_LIMIT):
    return pltpu.CompilerParams(dimension_semantics=sem, vmem_limit_bytes=vmem)


def _dot(a, b):
    return jnp.dot(a, b, preferred_element_type=F32)


def _dot_nt(a, b):
    return lax.dot_general(a, b, (((1,), (1,)), ((), ())), preferred_element_type=F32)


def _silu(x):
    return x * (1.0 / (1.0 + jnp.exp(-x)))


def _row_chunks(tm):
    mc = min(MM_CHUNK, tm)
    return [(r, mc) for r in range(0, tm, mc)]


def _ada_kernel(cond_ref, w_ref, b_ref, o_ref):
    s = _silu(cond_ref[...]).astype(BF16)
    o_ref[0] = _dot(s, w_ref[0].astype(BF16)) + b_ref[0]


def ada_mods(cond8, ada_w, ada_b):
    L, D, ND = ada_w.shape
    tn = min(512, ND)
    return pl.pallas_call(
        _ada_kernel,
        grid=(L, ND // tn),
        in_specs=[pl.BlockSpec((8, D), lambda l, j: (0, 0)),
                  pl.BlockSpec((1, D, tn), lambda l, j: (l, 0, j)),
                  pl.BlockSpec((1, 1, tn), lambda l, j: (l, 0, j))],
        out_specs=pl.BlockSpec((1, 8, tn), lambda l, j: (l, 0, j)),
        out_shape=jax.ShapeDtypeStruct((L, 8, ND), F32),
        compiler_params=_cp(("parallel", "parallel")),
        name="ada",
    )(cond8, ada_w, ada_b.reshape(L, 1, ND))


def _norm_mod(x, g, sh, sc):
    ms = jnp.mean(x * x, axis=-1, keepdims=True)
    y = x * lax.rsqrt(ms + NORM_EPS) * g
    return y * (1.0 + sc) + sh


def _normmod_kernel(x_ref, g_ref, sh_ref, sc_ref, o_ref, *, row):
    r = pl.program_id(0) if row is None else row
    sh = sh_ref[0, pl.ds(r, 1), :]
    sc = sc_ref[0, pl.ds(r, 1), :]
    o_ref[0] = _norm_mod(x_ref[0], g_ref[0], sh, sc).astype(o_ref.dtype)


def _mod_spec(D, layer, chunk):
    return pl.BlockSpec((1, 8, D), lambda b, i: (layer, 0, chunk))


def normmod(x3, g3, mods, layer, sh_chunk, row, tm):
    B, n, D = x3.shape
    tm = min(tm, n)
    return pl.pallas_call(
        functools.partial(_normmod_kernel, row=row),
        grid=(B, n // tm),
        in_specs=[pl.BlockSpec((1, tm, D), lambda b, i: (b, i, 0)),
                  pl.BlockSpec((1, 1, D), lambda b, i: (layer, 0, 0)),
                  _mod_spec(D, layer, sh_chunk),
                  _mod_spec(D, layer, sh_chunk + 1)],
        out_specs=pl.BlockSpec((1, tm, D), lambda b, i: (b, i, 0)),
        out_shape=jax.ShapeDtypeStruct((B, n, D), BF16),
        compiler_params=_cp(("parallel", "parallel")),
        name="normmod",
    )(x3, g3, mods, mods)


def _qkv_kernel(a_ref, w_ref, gain_ref, o_ref, *, norm, transposed, dh):
    wb = w_ref[...].astype(BF16)
    tn = wb.shape[1]
    for r0, mc in _row_chunks(a_ref.shape[0]):
        acc = _dot(a_ref[r0:r0 + mc, :], wb)
        for c in range(tn // dh):
            y = acc[:, c * dh:(c + 1) * dh]
            if norm:
                ms = jnp.mean(y * y, axis=-1, keepdims=True)
                y = y * lax.rsqrt(ms + NORM_EPS) * gain_ref[:, c * dh:(c + 1) * dh]
            if transposed:
                o_ref[c, :, r0:r0 + mc] = y.T.astype(o_ref.dtype)
            else:
                o_ref[c, r0:r0 + mc, :] = y.astype(o_ref.dtype)


def qkv_proj(a, w, gain, col0, ncols, dh, norm, transposed, tm, tn):
    M, D = a.shape
    tm = min(tm, M)
    off = col0 // tn
    if transposed:
        out_spec = pl.BlockSpec((tn // dh, dh, tm), lambda i, j: (j, 0, i))
        out_shape = jax.ShapeDtypeStruct((ncols // dh, dh, M), BF16)
    else:
        out_spec = pl.BlockSpec((tn // dh, tm, dh), lambda i, j: (j, i, 0))
        out_shape = jax.ShapeDtypeStruct((ncols // dh, M, dh), BF16)
    return pl.pallas_call(
        functools.partial(_qkv_kernel, norm=norm, transposed=transposed, dh=dh),
        grid=(M // tm, ncols // tn),
        in_specs=[pl.BlockSpec((tm, D), lambda i, j: (i, 0)),
                  pl.BlockSpec((D, tn), lambda i, j: (0, j + off)),
                  pl.BlockSpec((1, tn), lambda i, j: (0, j + off))],
        out_specs=out_spec,
        out_shape=out_shape,
        compiler_params=_cp(("parallel", "arbitrary")),
        name="qkv",
    )(a, w, gain)


def _attn_bias_variants():
    out = []
    for var in range(3):
        tab = {}
        for i in range(ATTN_QROWS):
            for j in range(ATTN_KROWS):
                if var == 0:
                    valid, dr = j < NA_KH, j - i + NA_KH - 1
                elif var == 1:
                    valid, dr = i <= j < i + NA_KH, j - i + NA_KH // 2 - 1
                else:
                    valid, dr = j >= ATTN_KROWS - NA_KH, j - i - 1
                tab[(i, j)] = (valid, dr)
        out.append(tab)
    return out


def _attn_kernel(rpb_ref, q_ref, kt_ref, v_ref, kct_ref, vc_ref, o_ref, bias_ref, sl_ref, sc_ref,
                 pl_ref, pc_ref, den_ref, *, gw, rows):
    nblk = rows // ATTN_QROWS
    qn = ATTN_QROWS * gw
    kn = ATTN_KROWS * gw

    @pl.when(pl.program_id(1) == 0)
    def _build_bias():
        qc = lax.broadcasted_iota(I32, (gw, gw), 0)
        kc = lax.broadcasted_iota(I32, (gw, gw), 1)
        cs = jnp.clip(qc - NA_KW // 2, 0, gw - NA_KW)
        col_mask = jnp.where((kc >= cs) & (kc < cs + NA_KW), 0.0, NEG_INF).astype(F32)
        tiles = []
        for dr in range(2 * NA_KH - 1):
            r = jnp.broadcast_to(rpb_ref[0, dr:dr + 1, :], (gw, LANE))
            t = pltpu.roll(r, LANE - (NA_KW - 1), 1, stride=1, stride_axis=0)
            tiles.append(t[:, :gw] + col_mask)
        neg = jnp.full((gw, gw), NEG_INF, F32)
        for var, tab in enumerate(_attn_bias_variants()):
            for (i, j), (valid, dr) in tab.items():
                bias_ref[var, i * gw:(i + 1) * gw, j * gw:(j + 1) * gw] = tiles[dr] if valid else neg

    def geom(blk):
        if isinstance(blk, int):
            r0 = blk * ATTN_QROWS
            start = min(max(r0 - NA_KH // 2, 0), rows - ATTN_KROWS)
            var = 0 if blk == 0 else (2 if blk == nblk - 1 else 1)
            return r0 * gw, start * gw, var
        r0 = blk * ATTN_QROWS
        start = jnp.clip(r0 - NA_KH // 2, 0, rows - ATTN_KROWS)
        var = jnp.where(blk == 0, 0, jnp.where(blk == nblk - 1, 2, 1))
        return pl.multiple_of(r0 * gw, qn), pl.multiple_of(start * gw, LANE), var

    def scores(blk, slot):
        q0, k0, var = geom(blk)
        q = q_ref[pl.ds(q0, qn), :]
        sl_ref[slot] = _dot(q, kt_ref[:, pl.ds(k0, kn)]) + bias_ref[var]
        sc_ref[slot] = _dot(q, kct_ref[...])

    def softmax(slot):
        s_lat = sl_ref[slot]
        s_ctx = sc_ref[slot]
        m = jnp.maximum(jnp.max(s_lat, axis=-1, keepdims=True), jnp.max(s_ctx, axis=-1, keepdims=True))
        p_lat = jnp.exp(s_lat - m)
        p_ctx = jnp.exp(s_ctx - m)
        den_ref[slot] = jnp.sum(p_lat, axis=-1, keepdims=True) + jnp.sum(p_ctx, axis=-1, keepdims=True)
        pl_ref[slot] = p_lat.astype(BF16)
        pc_ref[slot] = p_ctx.astype(BF16)

    def values(blk, slot):
        q0, k0, _ = geom(blk)
        o = _dot(pl_ref[slot], v_ref[pl.ds(k0, kn), :]) + _dot(pc_ref[slot], vc_ref[...])
        o_ref[pl.ds(q0, qn), :] = (o * (1.0 / den_ref[slot])).astype(o_ref.dtype)

    scores(0, 0)
    scores(1, 1)
    softmax(0)

    def body(it, carry):
        i = 2 + 2 * it
        scores(i, 0)
        softmax(1)
        values(i - 2, 0)
        scores(i + 1, 1)
        softmax(0)
        values(i - 1, 1)
        return carry

    lax.fori_loop(0, (nblk - 2) // 2, body, 0)
    softmax(1)
    values(nblk - 2, 0)
    values(nblk - 1, 1)


def attention(q, kt, v, kct, vc, rpb_pad, cfg):
    B, N, D, NC, H, GW = cfg.B, cfg.N, cfg.D, cfg.NC, cfg.H, cfg.GW
    dh = D // H
    rows = N // GW
    assert dh == LANE and rows >= ATTN_KROWS and rows % (2 * ATTN_QROWS) == 0 and GW >= NA_KW
    assert (ATTN_QROWS * GW) % LANE == 0 and (NA_KH // 2 * GW) % LANE == 0 and ((rows - ATTN_KROWS) * GW) % LANE == 0
    return pl.pallas_call(
        functools.partial(_attn_kernel, gw=GW, rows=rows),
        grid=(H, B),
        in_specs=[pl.BlockSpec((1, 2 * NA_KH, LANE), lambda h, b: (h, 0, 0)),
                  pl.BlockSpec((None, N, dh), lambda h, b: (h, b, 0)),
                  pl.BlockSpec((None, dh, N), lambda h, b: (h, 0, b)),
                  pl.BlockSpec((None, N, dh), lambda h, b: (h, b, 0)),
                  pl.BlockSpec((None, dh, NC), lambda h, b: (h, 0, b)),
                  pl.BlockSpec((None, NC, dh), lambda h, b: (h, b, 0))],
        out_specs=pl.BlockSpec((N, dh), lambda h, b: (b, h)),
        out_shape=jax.ShapeDtypeStruct((B * N, D), BF16),
        scratch_shapes=[pltpu.VMEM((3, ATTN_QROWS * GW, ATTN_KROWS * GW), F32),
                        pltpu.VMEM((2, ATTN_QROWS * GW, ATTN_KROWS * GW), F32),
                        pltpu.VMEM((2, ATTN_QROWS * GW, NC), F32),
                        pltpu.VMEM((2, ATTN_QROWS * GW, ATTN_KROWS * GW), BF16),
                        pltpu.VMEM((2, ATTN_QROWS * GW, NC), BF16),
                        pltpu.VMEM((2, ATTN_QROWS * GW, 1), F32)],
        compiler_params=_cp(("arbitrary", "arbitrary")),
        name="attn",
    )(rpb_pad, q, kt, v, kct, vc)


def _proj_res_kernel(a_ref, w_ref, x_ref, g_ref, *rest, tiles_per_batch, scaled):
    if scaled:
        ps_ref, o_ref = rest
    else:
        (o_ref,) = rest
    wb = w_ref[0].astype(BF16)
    b = pl.program_id(0) // tiles_per_batch
    g = g_ref[0, pl.ds(b, 1), :]
    for r0, mc in _row_chunks(a_ref.shape[0]):
        y = _dot(a_ref[r0:r0 + mc, :], wb)
        if scaled:
            y = y * ps_ref[...]
        o_ref[r0:r0 + mc, :] = x_ref[r0:r0 + mc, :] + g * y


def proj_residual(a, w3, x2, mods, layer, gate_chunk, n_per_batch, pscale, tm, tn):
    M = a.shape[0]
    G, K, KO = w3.shape
    Dout = G * KO
    tm = min(tm, n_per_batch)
    tn = min(tn, KO)
    nj = KO // tn
    scaled = pscale is not None
    in_specs = [pl.BlockSpec((tm, K), lambda i, g, j: (i, g)),
                pl.BlockSpec((1, K, tn), lambda i, g, j: (g, 0, j)),
                pl.BlockSpec((tm, tn), lambda i, g, j: (i, g * nj + j)),
                pl.BlockSpec((1, 8, tn), lambda i, g, j: (layer, 0, gate_chunk * (Dout // tn) + g * nj + j))]
    args = [a, w3, x2, mods]
    if scaled:
        in_specs.append(pl.BlockSpec((1, tn), lambda i, g, j: (0, g * nj + j)))
        args.append(pscale)
    return pl.pallas_call(
        functools.partial(_proj_res_kernel, tiles_per_batch=n_per_batch // tm, scaled=scaled),
        grid=(M // tm, G, nj),
        in_specs=in_specs,
        out_specs=pl.BlockSpec((tm, tn), lambda i, g, j: (i, g * nj + j)),
        out_shape=jax.ShapeDtypeStruct((M, Dout), F32),
        compiler_params=_cp(("parallel", "arbitrary", "arbitrary")),
        name="proj_res",
    )(*args)


def _split_bf16(x):
    hi = x.astype(BF16)
    lo = (x - hi.astype(F32)).astype(BF16)
    return hi, lo


def _ffn_norm_kernel(x_ref, g_ref, sh_ref, sc_ref, wr_ref, hfx_ref, *, E):
    b = pl.program_id(0)
    D = x_ref.shape[2]
    hf = _norm_mod(x_ref[0], g_ref[0], sh_ref[0, pl.ds(b, 1), :], sc_ref[0, pl.ds(b, 1), :])
    bits = pltpu.bitcast(hf.astype(BF16).astype(F32), U32)
    hfx_ref[0, :, 0:D // 2] = (bits[:, 0:D // 2] >> 16) | bits[:, D // 2:D]
    h_hi, h_lo = _split_bf16(hf)
    w_hi, w_lo = _split_bf16(wr_ref[...])
    lg = _dot(h_hi, w_hi) + (_dot(h_hi, w_lo) + _dot(h_lo, w_hi))
    lane = lax.broadcasted_iota(I32, (1, LANE), 1)
    lg = jnp.where(lane < E, lg, NEG_INF)
    ex = jnp.exp(lg - jnp.max(lg, axis=-1, keepdims=True))
    aff = ex / jnp.sum(ex, axis=-1, keepdims=True)
    hfx_ref[0, :, D // 2:D // 2 + LANE] = pltpu.bitcast(aff, U32)


def ffn_norm_router(x3, g3, mods, layer, wr_pad, E, tm):
    B, N, D = x3.shape
    tm = min(tm, N)
    return pl.pallas_call(
        functools.partial(_ffn_norm_kernel, E=E),
        grid=(B, N // tm),
        in_specs=[pl.BlockSpec((1, tm, D), lambda b, i: (b, i, 0)),
                  pl.BlockSpec((1, 1, D), lambda b, i: (layer, 0, 0)),
                  _mod_spec(D, layer, 3),
                  _mod_spec(D, layer, 4),
                  pl.BlockSpec((D, LANE), lambda b, i: (0, 0))],
        out_specs=pl.BlockSpec((1, tm, D // 2 + LANE), lambda b, i: (b, i, 0)),
        out_shape=jax.ShapeDtypeStruct((B, N, D // 2 + LANE), U32),
        compiler_params=_cp(("parallel", "parallel")),
        name="ffn_norm",
    )(x3, g3, mods, mods, wr_pad)


def _route_kernel(aff_ref, idx_ref, posq_ref, starts_ref, t_ref, cum_ref, *, E, cap, N):
    b = pl.program_id(0)
    nblk = N // TOK_TILE
    logn = int(math.log2(N))
    for k in range(N // LANE):
        t_ref[:, k * LANE:(k + 1) * LANE] = pltpu.bitcast(aff_ref[0, k * LANE:(k + 1) * LANE, :], F32).T
    bits = pltpu.bitcast(t_ref[0:E, :], I32)
    tok = lax.broadcasted_iota(I32, (1, N), 1)

    def count(pred):
        return jnp.sum(jnp.where(pred, 1, 0), axis=1, keepdims=True)

    def thr_step(i, thr):
        cand = thr | (jnp.int32(1) << (30 - i))
        return jnp.where(count(bits >= cand) >= cap, cand, thr)

    thr = lax.fori_loop(0, 31, thr_step, jnp.zeros((E, 1), I32))
    gt = bits > thr
    eq = bits == thr
    need = cap - count(gt)

    def tie_step(i, ans):
        cand = ans | (jnp.int32(1) << (logn - 1 - i))
        return jnp.where(count(eq & (tok < cand)) < need, cand, ans)

    last = lax.fori_loop(0, logn, tie_step, jnp.zeros((E, 1), I32))
    sel = gt | (eq & (tok <= last))
    sel_f = jnp.where(sel, 1.0, 0.0).astype(F32)

    ri = lax.broadcasted_iota(I32, (TOK_TILE, TOK_TILE), 0)
    ci = lax.broadcasted_iota(I32, (TOK_TILE, TOK_TILE), 1)
    tri = jnp.where(ri <= ci, 1.0, 0.0).astype(BF16)
    lane_s = lax.broadcasted_iota(I32, (1, LANE), 1)
    carry = jnp.zeros((E, 1), F32)
    starts = jnp.zeros((E, LANE), F32)
    for k in range(nblk):
        m = sel_f[:, k * TOK_TILE:(k + 1) * TOK_TILE]
        starts = jnp.where(lane_s == k, carry, starts)
        cum_ref[:, k * TOK_TILE:(k + 1) * TOK_TILE] = _dot(m.astype(BF16), tri) + carry
        carry = carry + jnp.sum(m, axis=1, keepdims=True)
    starts_ref[0] = jnp.where(lane_s == nblk, carry, starts).astype(I32)

    lane_c = lax.broadcasted_iota(I32, (1, cap), 1)

    def slot_step(c, out):
        cnt = jnp.sum(jnp.where(cum_ref[...] <= lax.convert_element_type(c, F32), 1, 0), axis=1, keepdims=True)
        return jnp.where(lane_c == c, cnt, out)

    idx = lax.fori_loop(0, cap, slot_step, jnp.zeros((E, cap), I32), unroll=4)
    idx_ref[0] = idx + b * N

    t_ref[0:E, :] = jnp.where(sel, cum_ref[...], 0.0)
    t_ref[E:LANE, :] = jnp.zeros((LANE - E, N), F32)
    for k in range(N // LANE):
        posq_ref[0, k * LANE:(k + 1) * LANE, :] = t_ref[:, k * LANE:(k + 1) * LANE].T


def route(hfx, cfg):
    B, N, D, E, cap = cfg.B, cfg.N, cfg.D, cfg.E, cfg.CAP
    nblk = N // TOK_TILE
    assert N % TOK_TILE == 0 and cap % LANE == 0 and (1 << int(math.log2(N))) == N
    assert E % 8 == 0 and nblk < LANE
    return pl.pallas_call(
        functools.partial(_route_kernel, E=E, cap=cap, N=N),
        grid=(B,),
        in_specs=[pl.BlockSpec((1, N, LANE), lambda b: (b, 0, D // 2 // LANE))],
        out_specs=[pl.BlockSpec((1, E, cap), lambda b: (b, 0, 0)),
                   pl.BlockSpec((1, N, LANE), lambda b: (b, 0, 0)),
                   pl.BlockSpec((1, E, LANE), lambda b: (b, 0, 0))],
        out_shape=[jax.ShapeDtypeStruct((B, E, cap), I32),
                   jax.ShapeDtypeStruct((B, N, LANE), F32),
                   jax.ShapeDtypeStruct((B, E, LANE), I32)],
        scratch_shapes=[pltpu.VMEM((LANE, N), F32), pltpu.VMEM((E, N), F32)],
        compiler_params=_cp(("arbitrary",)),
        name="route",
    )(hfx)


def _expert_kernel(idx_ref, hfx_hbm, wg_ref, wu_ref, wd_ref, o_ref, xs_ref, ga_ref, stg_ref, hid_ref,
                   ag_ref, au_ref, sem, *, nf, nk, tf, nn):
    e = pl.program_id(0)
    s = pl.program_id(1)
    ne = pl.num_programs(0)
    rows, D = xs_ref.shape
    dk = D // nk
    rpt = rows // nn

    def row_copy(expert, r):
        tok = idx_ref[expert * rows + r]
        return pltpu.make_async_copy(hfx_hbm.at[pl.ds(tok, 1), :], stg_ref.at[pl.ds(r, 1), :], sem.at[0])

    @pl.when((e == 0) & (s == 0))
    def _():
        def body(r, carry):
            row_copy(0, r).start()
            return carry
        lax.fori_loop(0, rows, body, 0, unroll=8)

    @pl.when(s == 0)
    def _():
        pltpu.make_async_copy(hfx_hbm.at[pl.ds(0, rows), :], stg_ref, sem.at[0]).wait()
        for r0, mc in _row_chunks(rows):
            w = stg_ref[r0:r0 + mc, 0:D // 2]
            xs_ref[r0:r0 + mc, 0:D // 2] = pltpu.bitcast(w << 16, F32).astype(BF16)
            xs_ref[r0:r0 + mc, D // 2:D] = pltpu.bitcast(w & jnp.uint32(0xFFFF0000), F32).astype(BF16)
        ga_ref[...] = pltpu.bitcast(stg_ref[:, D // 2:D // 2 + LANE], F32)

    for kh in range(nk):
        @pl.when((s < nf * nk) & (s % nk == kh))
        def _(kh=kh):
            wg = wg_ref[0].astype(BF16)
            wu = wu_ref[0].astype(BF16)
            for r0, mc in _row_chunks(rows):
                x = xs_ref[r0:r0 + mc, kh * dk:(kh + 1) * dk]
                a = _dot(x, wg)
                u = _dot(x, wu)
                if kh > 0:
                    a = a + ag_ref[r0:r0 + mc, :]
                    u = u + au_ref[r0:r0 + mc, :]
                if kh < nk - 1:
                    ag_ref[r0:r0 + mc, :] = a
                    au_ref[r0:r0 + mc, :] = u
                else:
                    hid_ref[s // nk, r0:r0 + mc, :] = (_silu(a) * u).astype(BF16)

    @pl.when(s >= nf * nk)
    def _():
        nxt = jnp.where(e + 1 < ne, e + 1, 0)
        base = (s - nf * nk) * rpt
        for j in range(rpt):
            row_copy(nxt, base + j).start()
        wd = wd_ref[0].astype(BF16)
        lane = lax.broadcasted_iota(I32, (1, LANE), 1)
        for r0, mc in _row_chunks(rows):
            acc = _dot(hid_ref[0, r0:r0 + mc, :], wd[0:tf, :])
            for k in range(1, nf):
                acc = acc + _dot(hid_ref[k, r0:r0 + mc, :], wd[k * tf:(k + 1) * tf, :])
            gate = jnp.sum(jnp.where(lane == e, ga_ref[r0:r0 + mc, :], 0.0), axis=-1, keepdims=True)
            o_ref[r0:r0 + mc, :] = (acc * gate).astype(o_ref.dtype)

    @pl.when((e == ne - 1) & (s == pl.num_programs(1) - 1))
    def _():
        pltpu.make_async_copy(hfx_hbm.at[pl.ds(0, rows), :], stg_ref, sem.at[0]).wait()


def experts(idx_flat, hfx2, w_gate, w_up, w_down, layer, rows, tf, tn):
    _, E, D, F = w_gate.shape
    M = idx_flat.shape[0]
    DX = hfx2.shape[1]
    assert M == E * rows and DX == D // 2 + LANE
    tf = min(tf, F)
    tn = min(tn, D)
    nf, nn = F // tf, D // tn
    nk = EXPERT_KSPLIT
    n1 = nf * nk
    assert rows % nn == 0

    def w_in(e, s, idx):
        s1 = jnp.minimum(s, n1 - 1)
        return (layer, e, s1 % nk, s1 // nk)

    return pl.pallas_call(
        functools.partial(_expert_kernel, nf=nf, nk=nk, tf=tf, nn=nn),
        grid_spec=pltpu.PrefetchScalarGridSpec(
            num_scalar_prefetch=1,
            grid=(E, n1 + nn),
            in_specs=[pl.BlockSpec(memory_space=pl.ANY),
                      pl.BlockSpec((None, 1, D // nk, tf), w_in),
                      pl.BlockSpec((None, 1, D // nk, tf), w_in),
                      pl.BlockSpec((None, 1, F, tn), lambda e, s, idx: (layer, e, 0, jnp.maximum(s - n1, 0)))],
            out_specs=pl.BlockSpec((rows, tn), lambda e, s, idx: (e, jnp.maximum(s - n1, 0))),
            scratch_shapes=[pltpu.VMEM((rows, D), BF16),
                            pltpu.VMEM((rows, LANE), F32),
                            pltpu.VMEM((rows, DX), U32),
                            pltpu.VMEM((nf, rows, tf), BF16),
                            pltpu.VMEM((rows, tf), F32), pltpu.VMEM((rows, tf), F32),
                            pltpu.SemaphoreType.DMA((1,))]),
        out_shape=jax.ShapeDtypeStruct((M, D), BF16),
        compiler_params=_cp(("arbitrary", "arbitrary")),
        name="experts",
    )(idx_flat, hfx2, w_gate, w_up, w_down)


def _combine_kernel(st_ref, x_ref, g_ref, posq_ref, yg_hbm, o_ref, ssq_ref, ybuf, acc_ref, sem,
                    *, E, cap, B, nt):
    b = pl.program_id(0)
    t = pl.program_id(1)
    W = COMB_WIN
    KW = E * W
    step = b * nt + t
    slot = step % 2

    def tile_lo(bb, tt):
        base = (bb * (nt + 1) + tt) * E
        return [(st_ref[base + e] // BF16_ROWS) * BF16_ROWS for e in range(E)]

    def window_copy(bb, e, start, sl):
        row0 = pl.multiple_of((e * B + bb) * cap + start, BF16_ROWS)
        return pltpu.make_async_copy(yg_hbm.at[pl.ds(row0, W), :], ybuf.at[sl, pl.ds(e * W, W), :], sem.at[sl])

    def start_round0(bb, tt, sl):
        lo_n = tile_lo(bb, tt)
        for e in range(E):
            window_copy(bb, e, jnp.minimum(lo_n[e], cap - W), sl).start()

    @pl.when(step == 0)
    def _():
        start_round0(b, t, slot)

    @pl.when(step + 1 < B * nt)
    def _():
        start_round0((step + 1) // nt, (step + 1) % nt, 1 - slot)

    lo = tile_lo(b, t)
    base = (b * (nt + 1) + t) * E
    hi = [st_ref[base + E + e] for e in range(E)]
    nrounds = jnp.int32(1)
    for e in range(E):
        nrounds = jnp.maximum(nrounds, (hi[e] - lo[e] + (W - 1)) // W)

    qi = posq_ref[0].astype(I32)
    erow = lax.broadcasted_iota(I32, (LANE, KW), 0)
    ecol = lax.broadcasted_iota(I32, (LANE, KW), 1) // W
    expand = jnp.where(erow == ecol, 1.0, 0.0).astype(BF16)
    qe = (_dot((qi >> 5).astype(F32).astype(BF16), expand) * 32.0
          + _dot((qi & 31).astype(F32).astype(BF16), expand)).astype(I32) - 1
    lane_e = lax.broadcasted_iota(I32, (1, KW), 1) // W
    lane_k = lax.broadcasted_iota(I32, (1, KW), 1) % W

    def one_round(r, prefetched):
        lo_r = [lo[e] + r * W for e in range(E)]
        st = [jnp.minimum(lo_r[e], cap - W) for e in range(E)]
        if not prefetched:
            for e in range(E):
                window_copy(b, e, st[e], slot).start()
        stv = jnp.zeros((1, KW), I32)
        lov = jnp.zeros((1, KW), I32)
        for e in range(E):
            stv = jnp.where(lane_e == e, st[e], stv)
            lov = jnp.where(lane_e == e, lo_r[e], lov)
        onehot = (qe - stv == lane_k) & (qe >= lov) & (qe < lov + W)
        s = jnp.where(onehot, 1.0, 0.0).astype(BF16)
        for e in range(E):
            window_copy(b, e, st[e], slot).wait()
        return _dot(s, ybuf[slot])

    acc_ref[...] = one_round(0, True)

    def extra(r, carry):
        acc_ref[...] += one_round(r, False)
        return carry

    lax.fori_loop(1, nrounds, extra, 0)
    x2 = x_ref[...] + g_ref[0, pl.ds(b, 1), :] * acc_ref[...]
    o_ref[...] = x2
    ssq_ref[...] = jnp.sum(x2 * x2, axis=-1, keepdims=True)


def combine(starts_flat, x2d, mods, layer, posq, yg, cfg):
    B, N, D, E, cap = cfg.B, cfg.N, cfg.D, cfg.E, cfg.CAP
    nt = N // TOK_TILE
    assert cap >= COMB_WIN and cap % COMB_WIN == 0 and COMB_WIN % BF16_ROWS == 0 and cap <= 32 * 32
    return pl.pallas_call(
        functools.partial(_combine_kernel, E=E, cap=cap, B=B, nt=nt),
        grid_spec=pltpu.PrefetchScalarGridSpec(
            num_scalar_prefetch=1,
            grid=(B, nt),
            in_specs=[pl.BlockSpec((TOK_TILE, D), lambda b, t, st: (b * nt + t, 0)),
                      pl.BlockSpec((1, 8, D), lambda b, t, st: (layer, 0, 5)),
                      pl.BlockSpec((1, TOK_TILE, LANE), lambda b, t, st: (b, t, 0)),
                      pl.BlockSpec(memory_space=pl.ANY)],
            out_specs=[pl.BlockSpec((TOK_TILE, D), lambda b, t, st: (b * nt + t, 0)),
                       pl.BlockSpec((TOK_TILE, 1), lambda b, t, st: (b * nt + t, 0))],
            scratch_shapes=[pltpu.VMEM((2, E * COMB_WIN, D), BF16),
                            pltpu.VMEM((TOK_TILE, D), F32),
                            pltpu.SemaphoreType.DMA((2,))]),
        out_shape=[jax.ShapeDtypeStruct((B * N, D), F32),
                   jax.ShapeDtypeStruct((B * N, 1), F32)],
        compiler_params=_cp(("arbitrary", "arbitrary")),
        name="combine",
    )(starts_flat, x2d, mods, posq, yg)


def _pool_kernel(x_ref, ssq_ref, g_ref, sh_ref, sc_ref, o_ref, hs_ref, *, N, D, windows, chunks_per_group):
    b = pl.program_id(0)
    j = pl.program_id(1)
    P = max(windows) // 2
    cw = x_ref.shape[2]
    nblk = N // TOK_TILE
    g = g_ref[0]
    sh = sh_ref[0, pl.ds(b, 1), :]
    sc = sc_ref[0, pl.ds(b, 1), :]
    zeros = jnp.zeros((P, cw), F32)
    hs_ref[0:P, :] = zeros
    hs_ref[P + N:P + N + P, :] = zeros

    def fill(k, carry):
        r0 = pl.multiple_of(k * TOK_TILE, TOK_TILE)
        rstd = lax.rsqrt(ssq_ref[pl.ds(r0, TOK_TILE), :] * (1.0 / D) + NORM_EPS)
        y = x_ref[0, pl.ds(r0, TOK_TILE), :] * rstd * g
        hs_ref[pl.ds(pl.multiple_of(P + r0, 8), TOK_TILE), :] = y * (1.0 + sc) + sh
        return carry

    lax.fori_loop(0, nblk, fill, 0)

    for grp, w in enumerate(windows):
        half = w // 2

        @pl.when(j // chunks_per_group == grp)
        def _(half=half):
            def pool(k, carry):
                r0 = pl.multiple_of(k * TOK_TILE, TOK_TILE)
                big = hs_ref[pl.ds(r0, TOK_TILE + 2 * P), :]
                nrow = TOK_TILE + 2 * P
                fwd = big
                span = 1
                while span < 2 * half:
                    fwd = fwd + pltpu.roll(fwd, nrow - span, 0)
                    span *= 2
                acc = fwd[P - half:P - half + TOK_TILE]
                tk = lax.broadcasted_iota(I32, (TOK_TILE, 1), 0) + r0
                cnt = (jnp.minimum(tk + half, N) - jnp.maximum(tk - half, 0)).astype(F32)
                o_ref[0, pl.ds(r0, TOK_TILE), :] = (acc / cnt - big[P:P + TOK_TILE]).astype(o_ref.dtype)
                return carry

            lax.fori_loop(0, nblk, pool, 0)


def pool_mixer_input(x3, ssq, g3, mods, layer, cfg, cw):
    B, N, D = x3.shape
    dg = D // cfg.G
    cw = min(cw, dg)
    P = max(cfg.windows) // 2
    nj = D // cw

    def mspec(chunk):
        return pl.BlockSpec((1, 8, cw), lambda b, j: (layer, 0, chunk * nj + j))

    return pl.pallas_call(
        functools.partial(_pool_kernel, N=N, D=D, windows=cfg.windows, chunks_per_group=dg // cw),
        grid=(B, nj),
        in_specs=[pl.BlockSpec((1, N, cw), lambda b, j: (b, 0, j)),
                  pl.BlockSpec((N, 1), lambda b, j: (b, 0)),
                  pl.BlockSpec((1, 1, cw), lambda b, j: (layer, 0, j)),
                  mspec(0), mspec(1)],
        out_specs=pl.BlockSpec((1, N, cw), lambda b, j: (b, 0, j)),
        out_shape=jax.ShapeDtypeStruct((B, N, D), BF16),
        scratch_shapes=[pltpu.VMEM((N + 2 * P, cw), F32)],
        compiler_params=_cp(("parallel", "parallel")),
        name="pool",
    )(x3, ssq, g3, mods, mods)


def moe_block(x2d, mods, layer, g_ffn3, w_router, w_gate, w_up, w_down, cfg):
    B, N, D, E, cap = cfg.B, cfg.N, cfg.D, cfg.E, cfg.CAP
    nt = N // TOK_TILE
    wr_pad = jnp.pad(w_router, ((0, 0), (0, LANE - E)))
    hfx = ffn_norm_router(x2d.reshape(B, N, D), g_ffn3, mods, layer, wr_pad, E, tm=512)
    idx, posq, starts = route(hfx, cfg)
    idx_flat = jnp.transpose(idx, (1, 0, 2)).reshape(-1)
    starts_flat = jnp.transpose(starts[:, :, :nt + 1], (0, 2, 1)).reshape(-1)
    yg = experts(idx_flat, hfx.reshape(B * N, D // 2 + LANE), w_gate, w_up, w_down, layer, B * cap,
                 tf=512, tn=512)
    return combine(starts_flat, x2d, mods, layer, posq, yg, cfg)


def forward(cfg, x, c, ctx, c_ctx, ada_w, ada_b, norm_mix_g, norm_ffn_g, na_w_qkv, na_q_gain,
            na_k_gain, na_rpb, na_w_out, pool_w, pool_scale, moe_w_router, moe_w_gate,
            moe_w_up, moe_w_down):
    B, N, D, NC, H = cfg.B, cfg.N, cfg.D, cfg.NC, cfg.H
    dh = D // H
    L = ada_w.shape[0]
    assert L == 2 and B + 1 <= 8
    cond8 = jnp.concatenate([c, c_ctx[None, :], jnp.zeros((8 - B - 1, D), F32)], axis=0)
    mods = ada_mods(cond8, ada_w, ada_b)
    gmix3 = norm_mix_g.reshape(L, 1, D)
    gffn3 = norm_ffn_g.reshape(L, 1, D)

    h = normmod(x, gmix3, mods, 0, 0, None, tm=512).reshape(B * N, D)
    hc = normmod(ctx, gmix3, mods, 0, 0, B, tm=512).reshape(B * NC, D)
    scale = 1.0 / math.sqrt(dh)
    gain = jnp.concatenate([jnp.tile(na_q_gain[0] * scale, H), jnp.tile(na_k_gain[0], H),
                            jnp.ones((D,), F32)])[None, :]
    w_qkv = na_w_qkv[0]
    q = qkv_proj(h, w_qkv, gain, 0, D, dh, True, False, tm=2048, tn=256)
    kt = qkv_proj(h, w_qkv, gain, D, D, dh, True, True, tm=2048, tn=256)
    v = qkv_proj(h, w_qkv, gain, 2 * D, D, dh, False, False, tm=2048, tn=256)
    kct = qkv_proj(hc, w_qkv, gain, D, D, dh, True, True, tm=1024, tn=256)
    vc = qkv_proj(hc, w_qkv, gain, 2 * D, D, dh, False, False, tm=1024, tn=256)
    rpb_pad = jnp.pad(na_rpb[0], ((0, 0), (0, 1), (0, LANE - (2 * NA_KW - 1))))
    o = attention(q, kt, v, kct, vc, rpb_pad, cfg)
    x1 = proj_residual(o, na_w_out[0][None], x.reshape(B * N, D), mods, 0, 2, N, None, tm=2048, tn=256)
    x2, ssq = moe_block(x1, mods, 0, gffn3, moe_w_router[0], moe_w_gate, moe_w_up, moe_w_down, cfg)

    pooled = pool_mixer_input(x2.reshape(B, N, D), ssq, gmix3, mods, 1, cfg, cw=256).reshape(B * N, D)
    x3 = proj_residual(pooled, pool_w[0], x2, mods, 1, 2, N, pool_scale[0][None, :], tm=2048, tn=1024)
    x4, _ = moe_block(x3, mods, 1, gffn3, moe_w_router[1], moe_w_gate, moe_w_up, moe_w_down, cfg)
    return x4.reshape(B, N, D)


def kernel(x, c, ctx, c_ctx, ada_w, ada_b, norm_mix_g, norm_ffn_g, na_w_qkv, na_q_gain, na_k_gain, na_rpb, na_w_out, pool_w, pool_scale, moe_w_router, moe_w_gate, moe_w_up, moe_w_down):
    B, N, D = x.shape
    E, _, F = moe_w_gate.shape[1:]
    G = pool_w.shape[1]
    cfg = Cfg(B=B, N=N, D=D, NC=ctx.shape[1], H=na_rpb.shape[1], GW=64, E=E,
              CAP=max(1, 2 * N // E), F=F, G=G, windows=(2, 4, 8, 16), n_ada=6)
    return forward(cfg, x, c, ctx, c_ctx, ada_w, ada_b, norm_mix_g, norm_ffn_g, na_w_qkv, na_q_gain,
                   na_k_gain, na_rpb, na_w_out, pool_w, pool_scale, moe_w_router, moe_w_gate,
                   moe_w_up, moe_w_down)
```

```python
import functools
import math
from typing import NamedTuple

import jax
import jax.numpy as jnp
from jax import lax
from jax.experimental import pallas as pl
from jax.experimental.pallas import tpu as pltpu

F32 = jnp.float32
BF16 = jnp.bfloat16
I32 = jnp.int32
U32 = jnp.uint32

LANE = 128
NORM_EPS = 1e-6
NEG_INF = -1e30
NA_KH = 8
NA_KW = 16
ATTN_QROWS = 4
ATTN_KROWS = ATTN_QROWS + NA_KH
TOK_TILE = 256
COMB_WIN = 64
BF16_ROWS = 16
MM_CHUNK = 512
VMEM_LIMIT = 56 * 1024 * 1024


class Cfg(NamedTuple):
    B: int
    N: int
    D: int
    NC: int
    H: int
    GW: int
    E: int
    CAP: int
    F: int
    G: int
    windows: tuple
    n_ada: int


def _cp(sem, vmem=VMEM_LIMIT):
    return pltpu.CompilerParams(dimension_semantics=sem, vmem_limit_bytes=vmem)


def _dot(a, b):
    return jnp.dot(a, b, preferred_element_type=F32)


def _dot_nt(a, b):
    return lax.dot_general(a, b, (((1,), (1,)), ((), ())), preferred_element_type=F32)


def _silu(x):
    return x * (1.0 / (1.0 + jnp.exp(-x)))


def _row_chunks(tm):
    mc = min(MM_CHUNK, tm)
    return [(r, mc) for r in range(0, tm, mc)]


def _ada_kernel(cond_ref, w_ref, b_ref, o_ref):
    s = _silu(cond_ref[...]).astype(BF16)
    o_ref[0] = _dot(s, w_ref[0].astype(BF16)) + b_ref[0]


def ada_mods(cond8, ada_w, ada_b):
    L, D, ND = ada_w.shape
    tn = min(512, ND)
    return pl.pallas_call(
        _ada_kernel,
        grid=(L, ND // tn),
        in_specs=[pl.BlockSpec((8, D), lambda l, j: (0, 0)),
                  pl.BlockSpec((1, D, tn), lambda l, j: (l, 0, j)),
                  pl.BlockSpec((1, 1, tn), lambda l, j: (l, 0, j))],
        out_specs=pl.BlockSpec((1, 8, tn), lambda l, j: (l, 0, j)),
        out_shape=jax.ShapeDtypeStruct((L, 8, ND), F32),
        compiler_params=_cp(("parallel", "parallel")),
        name="ada",
    )(cond8, ada_w, ada_b.reshape(L, 1, ND))


def _norm_mod(x, g, sh, sc):
    ms = jnp.mean(x * x, axis=-1, keepdims=True)
    y = x * lax.rsqrt(ms + NORM_EPS) * g
    return y * (1.0 + sc) + sh


def _normmod_kernel(x_ref, g_ref, sh_ref, sc_ref, o_ref, *, row):
    r = pl.program_id(0) if row is None else row
    sh = sh_ref[0, pl.ds(r, 1), :]
    sc = sc_ref[0, pl.ds(r, 1), :]
    o_ref[0] = _norm_mod(x_ref[0], g_ref[0], sh, sc).astype(o_ref.dtype)


def _mod_spec(D, layer, chunk):
    return pl.BlockSpec((1, 8, D), lambda b, i: (layer, 0, chunk))


def normmod(x3, g3, mods, layer, sh_chunk, row, tm):
    B, n, D = x3.shape
    tm = min(tm, n)
    return pl.pallas_call(
        functools.partial(_normmod_kernel, row=row),
        grid=(B, n // tm),
        in_specs=[pl.BlockSpec((1, tm, D), lambda b, i: (b, i, 0)),
                  pl.BlockSpec((1, 1, D), lambda b, i: (layer, 0, 0)),
                  _mod_spec(D, layer, sh_chunk),
                  _mod_spec(D, layer, sh_chunk + 1)],
        out_specs=pl.BlockSpec((1, tm, D), lambda b, i: (b, i, 0)),
        out_shape=jax.ShapeDtypeStruct((B, n, D), BF16),
        compiler_params=_cp(("parallel", "parallel")),
        name="normmod",
    )(x3, g3, mods, mods)


def _qkv_kernel(a_ref, w_ref, gain_ref, o_ref, *, norm, transposed, dh):
    wb = w_ref[...].astype(BF16)
    tn = wb.shape[1]
    for r0, mc in _row_chunks(a_ref.shape[0]):
        acc = _dot(a_ref[r0:r0 + mc, :], wb)
        for c in range(tn // dh):
            y = acc[:, c * dh:(c + 1) * dh]
            if norm:
                ms = jnp.mean(y * y, axis=-1, keepdims=True)
                y = y * lax.rsqrt(ms + NORM_EPS) * gain_ref[:, c * dh:(c + 1) * dh]
            if transposed:
                o_ref[c, :, r0:r0 + mc] = y.T.astype(o_ref.dtype)
            else:
                o_ref[c, r0:r0 + mc, :] = y.astype(o_ref.dtype)


def qkv_proj(a, w, gain, col0, ncols, dh, norm, transposed, tm, tn):
    M, D = a.shape
    tm = min(tm, M)
    off = col0 // tn
    if transposed:
        out_spec = pl.BlockSpec((tn // dh, dh, tm), lambda i, j: (j, 0, i))
        out_shape = jax.ShapeDtypeStruct((ncols // dh, dh, M), BF16)
    else:
        out_spec = pl.BlockSpec((tn // dh, tm, dh), lambda i, j: (j, i, 0))
        out_shape = jax.ShapeDtypeStruct((ncols // dh, M, dh), BF16)
    return pl.pallas_call(
        functools.partial(_qkv_kernel, norm=norm, transposed=transposed, dh=dh),
        grid=(M // tm, ncols // tn),
        in_specs=[pl.BlockSpec((tm, D), lambda i, j: (i, 0)),
                  pl.BlockSpec((D, tn), lambda i, j: (0, j + off)),
                  pl.BlockSpec((1, tn), lambda i, j: (0, j + off))],
        out_specs=out_spec,
        out_shape=out_shape,
        compiler_params=_cp(("parallel", "arbitrary")),
        name="qkv",
    )(a, w, gain)


def _attn_bias_variants():
    out = []
    for var in range(3):
        tab = {}
        for i in range(ATTN_QROWS):
            for j in range(ATTN_KROWS):
                if var == 0:
                    valid, dr = j < NA_KH, j - i + NA_KH - 1
                elif var == 1:
                    valid, dr = i <= j < i + NA_KH, j - i + NA_KH // 2 - 1
                else:
                    valid, dr = j >= ATTN_KROWS - NA_KH, j - i - 1
                tab[(i, j)] = (valid, dr)
        out.append(tab)
    return out


def _attn_kernel(rpb_ref, q_ref, kt_ref, v_ref, kct_ref, vc_ref, o_ref, bias_ref, sl_ref, sc_ref,
                 pl_ref, pc_ref, den_ref, *, gw, rows):
    nblk = rows // ATTN_QROWS
    qn = ATTN_QROWS * gw
    kn = ATTN_KROWS * gw

    @pl.when(pl.program_id(1) == 0)
    def _build_bias():
        qc = lax.broadcasted_iota(I32, (gw, gw), 0)
        kc = lax.broadcasted_iota(I32, (gw, gw), 1)
        cs = jnp.clip(qc - NA_KW // 2, 0, gw - NA_KW)
        col_mask = jnp.where((kc >= cs) & (kc < cs + NA_KW), 0.0, NEG_INF).astype(F32)
        tiles = []
        for dr in range(2 * NA_KH - 1):
            r = jnp.broadcast_to(rpb_ref[0, dr:dr + 1, :], (gw, LANE))
            t = pltpu.roll(r, LANE - (NA_KW - 1), 1, stride=1, stride_axis=0)
            tiles.append(t[:, :gw] + col_mask)
        neg = jnp.full((gw, gw), NEG_INF, F32)
        for var, tab in enumerate(_attn_bias_variants()):
            for (i, j), (valid, dr) in tab.items():
                bias_ref[var, i * gw:(i + 1) * gw, j * gw:(j + 1) * gw] = tiles[dr] if valid else neg

    def geom(blk):
        if isinstance(blk, int):
            r0 = blk * ATTN_QROWS
            start = min(max(r0 - NA_KH // 2, 0), rows - ATTN_KROWS)
            var = 0 if blk == 0 else (2 if blk == nblk - 1 else 1)
            return r0 * gw, start * gw, var
        r0 = blk * ATTN_QROWS
        start = jnp.clip(r0 - NA_KH // 2, 0, rows - ATTN_KROWS)
        var = jnp.where(blk == 0, 0, jnp.where(blk == nblk - 1, 2, 1))
        return pl.multiple_of(r0 * gw, qn), pl.multiple_of(start * gw, LANE), var

    def scores(blk, slot):
        q0, k0, var = geom(blk)
        q = q_ref[pl.ds(q0, qn), :]
        sl_ref[slot] = _dot(q, kt_ref[:, pl.ds(k0, kn)]) + bias_ref[var]
        sc_ref[slot] = _dot(q, kct_ref[...])

    def softmax(slot):
        s_lat = sl_ref[slot]
        s_ctx = sc_ref[slot]
        m = jnp.maximum(jnp.max(s_lat, axis=-1, keepdims=True), jnp.max(s_ctx, axis=-1, keepdims=True))
        p_lat = jnp.exp(s_lat - m)
        p_ctx = jnp.exp(s_ctx - m)
        den_ref[slot] = jnp.sum(p_lat, axis=-1, keepdims=True) + jnp.sum(p_ctx, axis=-1, keepdims=True)
        pl_ref[slot] = p_lat.astype(BF16)
        pc_ref[slot] = p_ctx.astype(BF16)

    def values(blk, slot):
        q0, k0, _ = geom(blk)
        o = _dot(pl_ref[slot], v_ref[pl.ds(k0, kn), :]) + _dot(pc_ref[slot], vc_ref[...])
        o_ref[pl.ds(q0, qn), :] = (o * (1.0 / den_ref[slot])).astype(o_ref.dtype)

    scores(0, 0)
    scores(1, 1)
    softmax(0)

    def body(it, carry):
        i = 2 + 2 * it
        scores(i, 0)
        softmax(1)
        values(i - 2, 0)
        scores(i + 1, 1)
        softmax(0)
        values(i - 1, 1)
        return carry

    lax.fori_loop(0, (nblk - 2) // 2, body, 0)
    softmax(1)
    values(nblk - 2, 0)
    values(nblk - 1, 1)


def attention(q, kt, v, kct, vc, rpb_pad, cfg):
    B, N, D, NC, H, GW = cfg.B, cfg.N, cfg.D, cfg.NC, cfg.H, cfg.GW
    dh = D // H
    rows = N // GW
    assert dh == LANE and rows >= ATTN_KROWS and rows % (2 * ATTN_QROWS) == 0 and GW >= NA_KW
    assert (ATTN_QROWS * GW) % LANE == 0 and (NA_KH // 2 * GW) % LANE == 0 and ((rows - ATTN_KROWS) * GW) % LANE == 0
    return pl.pallas_call(
        functools.partial(_attn_kernel, gw=GW, rows=rows),
        grid=(H, B),
        in_specs=[pl.BlockSpec((1, 2 * NA_KH, LANE), lambda h, b: (h, 0, 0)),
                  pl.BlockSpec((None, N, dh), lambda h, b: (h, b, 0)),
                  pl.BlockSpec((None, dh, N), lambda h, b: (h, 0, b)),
                  pl.BlockSpec((None, N, dh), lambda h, b: (h, b, 0)),
                  pl.BlockSpec((None, dh, NC), lambda h, b: (h, 0, b)),
                  pl.BlockSpec((None, NC, dh), lambda h, b: (h, b, 0))],
        out_specs=pl.BlockSpec((N, dh), lambda h, b: (b, h)),
        out_shape=jax.ShapeDtypeStruct((B * N, D), BF16),
        scratch_shapes=[pltpu.VMEM((3, ATTN_QROWS * GW, ATTN_KROWS * GW), F32),
                        pltpu.VMEM((2, ATTN_QROWS * GW, ATTN_KROWS * GW), F32),
                        pltpu.VMEM((2, ATTN_QROWS * GW, NC), F32),
                        pltpu.VMEM((2, ATTN_QROWS * GW, ATTN_KROWS * GW), BF16),
                        pltpu.VMEM((2, ATTN_QROWS * GW, NC), BF16),
                        pltpu.VMEM((2, ATTN_QROWS * GW, 1), F32)],
        compiler_params=_cp(("arbitrary", "arbitrary")),
        name="attn",
    )(rpb_pad, q, kt, v, kct, vc)


def _proj_res_kernel(a_ref, w_ref, x_ref, g_ref, *rest, tiles_per_batch, scaled):
    if scaled:
        ps_ref, o_ref = rest
    else:
        (o_ref,) = rest
    wb = w_ref[0].astype(BF16)
    b = pl.program_id(0) // tiles_per_batch
    g = g_ref[0, pl.ds(b, 1), :]
    for r0, mc in _row_chunks(a_ref.shape[0]):
        y = _dot(a_ref[r0:r0 + mc, :], wb)
        if scaled:
            y = y * ps_ref[...]
        o_ref[r0:r0 + mc, :] = x_ref[r0:r0 + mc, :] + g * y


def proj_residual(a, w3, x2, mods, layer, gate_chunk, n_per_batch, pscale, tm, tn):
    M = a.shape[0]
    G, K, KO = w3.shape
    Dout = G * KO
    tm = min(tm, n_per_batch)
    tn = min(tn, KO)
    nj = KO // tn
    scaled = pscale is not None
    in_specs = [pl.BlockSpec((tm, K), lambda i, g, j: (i, g)),
                pl.BlockSpec((1, K, tn), lambda i, g, j: (g, 0, j)),
                pl.BlockSpec((tm, tn), lambda i, g, j: (i, g * nj + j)),
                pl.BlockSpec((1, 8, tn), lambda i, g, j: (layer, 0, gate_chunk * (Dout // tn) + g * nj + j))]
    args = [a, w3, x2, mods]
    if scaled:
        in_specs.append(pl.BlockSpec((1, tn), lambda i, g, j: (0, g * nj + j)))
        args.append(pscale)
    return pl.pallas_call(
        functools.partial(_proj_res_kernel, tiles_per_batch=n_per_batch // tm, scaled=scaled),
        grid=(M // tm, G, nj),
        in_specs=in_specs,
        out_specs=pl.BlockSpec((tm, tn), lambda i, g, j: (i, g * nj + j)),
        out_shape=jax.ShapeDtypeStruct((M, Dout), F32),
        compiler_params=_cp(("parallel", "arbitrary", "arbitrary")),
        name="proj_res",
    )(*args)


def _split_bf16(x):
    hi = x.astype(BF16)
    lo = (x - hi.astype(F32)).astype(BF16)
    return hi, lo


def _ffn_norm_kernel(x_ref, g_ref, sh_ref, sc_ref, wr_ref, hfx_ref, *, E):
    b = pl.program_id(0)
    D = x_ref.shape[2]
    hf = _norm_mod(x_ref[0], g_ref[0], sh_ref[0, pl.ds(b, 1), :], sc_ref[0, pl.ds(b, 1), :])
    bits = pltpu.bitcast(hf.astype(BF16).astype(F32), U32)
    hfx_ref[0, :, 0:D // 2] = (bits[:, 0:D // 2] >> 16) | bits[:, D // 2:D]
    h_hi, h_lo = _split_bf16(hf)
    w_hi, w_lo = _split_bf16(wr_ref[...])
    r = _dot(h_hi, jnp.concatenate([w_hi, w_lo], axis=1))
    lg = r[:, 0:LANE] + (r[:, LANE:2 * LANE] + _dot(h_lo, w_hi))
    lane = lax.broadcasted_iota(I32, (1, LANE), 1)
    lg = jnp.where(lane < E, lg, NEG_INF)
    ex = jnp.exp(lg - jnp.max(lg, axis=-1, keepdims=True))
    aff = ex / jnp.sum(ex, axis=-1, keepdims=True)
    hfx_ref[0, :, D // 2:D // 2 + LANE] = pltpu.bitcast(aff, U32)


def ffn_norm_router(x3, g3, mods, layer, wr_pad, E, tm):
    B, N, D = x3.shape
    tm = min(tm, N)
    return pl.pallas_call(
        functools.partial(_ffn_norm_kernel, E=E),
        grid=(B, N // tm),
        in_specs=[pl.BlockSpec((1, tm, D), lambda b, i: (b, i, 0)),
                  pl.BlockSpec((1, 1, D), lambda b, i: (layer, 0, 0)),
                  _mod_spec(D, layer, 3),
                  _mod_spec(D, layer, 4),
                  pl.BlockSpec((D, LANE), lambda b, i: (0, 0))],
        out_specs=pl.BlockSpec((1, tm, D // 2 + LANE), lambda b, i: (b, i, 0)),
        out_shape=jax.ShapeDtypeStruct((B, N, D // 2 + LANE), U32),
        compiler_params=_cp(("parallel", "parallel")),
        name="ffn_norm",
    )(x3, g3, mods, mods, wr_pad)


def _route_kernel(aff_ref, idx_ref, posq_ref, starts_ref, t_ref, cum_ref, *, E, cap, N):
    b = pl.program_id(0)
    nblk = N // TOK_TILE
    logn = int(math.log2(N))
    for k in range(N // LANE):
        t_ref[:, k * LANE:(k + 1) * LANE] = pltpu.bitcast(aff_ref[0, k * LANE:(k + 1) * LANE, :], F32).T
    bits = pltpu.bitcast(t_ref[0:E, :], I32)
    tok = lax.broadcasted_iota(I32, (1, N), 1)

    def count(pred):
        return jnp.sum(jnp.where(pred, 1, 0), axis=1, keepdims=True)

    def thr_step(i, thr):
        cand = thr | (jnp.int32(1) << (30 - i))
        return jnp.where(count(bits >= cand) >= cap, cand, thr)

    thr = lax.fori_loop(0, 31, thr_step, jnp.zeros((E, 1), I32))
    gt = bits > thr
    eq = bits == thr
    need = cap - count(gt)

    def tie_step(i, ans):
        cand = ans | (jnp.int32(1) << (logn - 1 - i))
        return jnp.where(count(eq & (tok < cand)) < need, cand, ans)

    last = lax.fori_loop(0, logn, tie_step, jnp.zeros((E, 1), I32))
    sel = gt | (eq & (tok <= last))
    sel_f = jnp.where(sel, 1.0, 0.0).astype(F32)

    ri = lax.broadcasted_iota(I32, (TOK_TILE, TOK_TILE), 0)
    ci = lax.broadcasted_iota(I32, (TOK_TILE, TOK_TILE), 1)
    tri = jnp.where(ri <= ci, 1.0, 0.0).astype(BF16)
    lane_s = lax.broadcasted_iota(I32, (1, LANE), 1)
    carry = jnp.zeros((E, 1), F32)
    starts = jnp.zeros((E, LANE), F32)
    for k in range(nblk):
        m = sel_f[:, k * TOK_TILE:(k + 1) * TOK_TILE]
        starts = jnp.where(lane_s == k, carry, starts)
        cum_ref[:, k * TOK_TILE:(k + 1) * TOK_TILE] = _dot(m.astype(BF16), tri) + carry
        carry = carry + jnp.sum(m, axis=1, keepdims=True)
    starts_ref[0] = jnp.where(lane_s == nblk, carry, starts).astype(I32)

    lane_c = lax.broadcasted_iota(I32, (1, cap), 1)

    def slot_step(c, out):
        cnt = jnp.sum(jnp.where(cum_ref[...] <= lax.convert_element_type(c, F32), 1, 0), axis=1, keepdims=True)
        return jnp.where(lane_c == c, cnt, out)

    idx = lax.fori_loop(0, cap, slot_step, jnp.zeros((E, cap), I32), unroll=4)
    idx_ref[0] = idx + b * N

    t_ref[0:E, :] = jnp.where(sel, cum_ref[...], 0.0)
    t_ref[E:LANE, :] = jnp.zeros((LANE - E, N), F32)
    for k in range(N // LANE):
        posq_ref[0, k * LANE:(k + 1) * LANE, :] = t_ref[:, k * LANE:(k + 1) * LANE].T


def route(hfx, cfg):
    B, N, D, E, cap = cfg.B, cfg.N, cfg.D, cfg.E, cfg.CAP
    nblk = N // TOK_TILE
    assert N % TOK_TILE == 0 and cap % LANE == 0 and (1 << int(math.log2(N))) == N
    assert E % 8 == 0 and nblk < LANE
    return pl.pallas_call(
        functools.partial(_route_kernel, E=E, cap=cap, N=N),
        grid=(B,),
        in_specs=[pl.BlockSpec((1, N, LANE), lambda b: (b, 0, D // 2 // LANE))],
        out_specs=[pl.BlockSpec((1, E, cap), lambda b: (b, 0, 0)),
                   pl.BlockSpec((1, N, LANE), lambda b: (b, 0, 0)),
                   pl.BlockSpec((1, E, LANE), lambda b: (b, 0, 0))],
        out_shape=[jax.ShapeDtypeStruct((B, E, cap), I32),
                   jax.ShapeDtypeStruct((B, N, LANE), F32),
                   jax.ShapeDtypeStruct((B, E, LANE), I32)],
        scratch_shapes=[pltpu.VMEM((LANE, N), F32), pltpu.VMEM((E, N), F32)],
        compiler_params=_cp(("arbitrary",)),
        name="route",
    )(hfx)


def _expert_kernel(idx_ref, hfx_hbm, wg_ref, wu_ref, wd_ref, o_ref, xs_ref, ga_ref, stg_ref, hid_ref,
                   sem, *, nf, tf, nn):
    e = pl.program_id(0)
    s = pl.program_id(1)
    ne = pl.num_programs(0)
    rows, D = xs_ref.shape
    rpt = rows // nn

    def row_copy(expert, r):
        tok = idx_ref[expert * rows + r]
        return pltpu.make_async_copy(hfx_hbm.at[pl.ds(tok, 1), :], stg_ref.at[pl.ds(r, 1), :], sem.at[0])

    @pl.when((e == 0) & (s == 0))
    def _():
        def body(r, carry):
            row_copy(0, r).start()
            return carry
        lax.fori_loop(0, rows, body, 0, unroll=8)

    @pl.when(s == 0)
    def _():
        pltpu.make_async_copy(hfx_hbm.at[pl.ds(0, rows), :], stg_ref, sem.at[0]).wait()
        for r0, mc in _row_chunks(rows):
            w = stg_ref[r0:r0 + mc, 0:D // 2]
            xs_ref[r0:r0 + mc, 0:D // 2] = pltpu.bitcast(w << 16, F32).astype(BF16)
            xs_ref[r0:r0 + mc, D // 2:D] = pltpu.bitcast(w & jnp.uint32(0xFFFF0000), F32).astype(BF16)
        ga_ref[...] = pltpu.bitcast(stg_ref[:, D // 2:D // 2 + LANE], F32)

    @pl.when(s < nf)
    def _():
        wg = wg_ref[0].astype(BF16)
        wu = wu_ref[0].astype(BF16)
        for r0, mc in _row_chunks(rows):
            x = xs_ref[r0:r0 + mc, :]
            hid_ref[s, r0:r0 + mc, :] = (_silu(_dot(x, wg)) * _dot(x, wu)).astype(BF16)

    @pl.when(s >= nf)
    def _():
        nxt = jnp.where(e + 1 < ne, e + 1, 0)
        base = (s - nf) * rpt
        for j in range(rpt):
            row_copy(nxt, base + j).start()
        wd = wd_ref[0].astype(BF16)
        lane = lax.broadcasted_iota(I32, (1, LANE), 1)
        for r0, mc in _row_chunks(rows):
            acc = _dot(hid_ref[0, r0:r0 + mc, :], wd[0:tf, :])
            for k in range(1, nf):
                acc = acc + _dot(hid_ref[k, r0:r0 + mc, :], wd[k * tf:(k + 1) * tf, :])
            gate = jnp.sum(jnp.where(lane == e, ga_ref[r0:r0 + mc, :], 0.0), axis=-1, keepdims=True)
            o_ref[r0:r0 + mc, :] = (acc * gate).astype(o_ref.dtype)

    @pl.when((e == ne - 1) & (s == pl.num_programs(1) - 1))
    def _():
        pltpu.make_async_copy(hfx_hbm.at[pl.ds(0, rows), :], stg_ref, sem.at[0]).wait()


def experts(idx_flat, hfx2, w_gate, w_up, w_down, layer, rows, tf, tn):
    _, E, D, F = w_gate.shape
    M = idx_flat.shape[0]
    DX = hfx2.shape[1]
    assert M == E * rows and DX == D // 2 + LANE
    tf = min(tf, F)
    tn = min(tn, D)
    nf, nn = F // tf, D // tn
    assert rows % nn == 0

    def w_in(e, s, idx):
        return (layer, e, 0, jnp.minimum(s, nf - 1))

    return pl.pallas_call(
        functools.partial(_expert_kernel, nf=nf, tf=tf, nn=nn),
        grid_spec=pltpu.PrefetchScalarGridSpec(
            num_scalar_prefetch=1,
            grid=(E, nf + nn),
            in_specs=[pl.BlockSpec(memory_space=pl.ANY),
                      pl.BlockSpec((None, 1, D, tf), w_in),
                      pl.BlockSpec((None, 1, D, tf), w_in),
                      pl.BlockSpec((None, 1, F, tn), lambda e, s, idx: (layer, e, 0, jnp.maximum(s - nf, 0)))],
            out_specs=pl.BlockSpec((rows, tn), lambda e, s, idx: (e, jnp.maximum(s - nf, 0))),
            scratch_shapes=[pltpu.VMEM((rows, D), BF16),
                            pltpu.VMEM((rows, LANE), F32),
                            pltpu.VMEM((rows, DX), U32),
                            pltpu.VMEM((nf, rows, tf), BF16),
                            pltpu.SemaphoreType.DMA((1,))]),
        out_shape=jax.ShapeDtypeStruct((M, D), BF16),
        compiler_params=_cp(("arbitrary", "arbitrary")),
        name="experts",
    )(idx_flat, hfx2, w_gate, w_up, w_down)


def _combine_kernel(st_ref, x_ref, g_ref, posq_ref, yg_hbm, o_ref, ssq_ref, ybuf, acc_ref, sem,
                    *, E, cap, B, nt):
    b = pl.program_id(0)
    t = pl.program_id(1)
    W = COMB_WIN
    KW = E * W
    step = b * nt + t
    slot = step % 2

    def tile_lo(bb, tt):
        base = (bb * (nt + 1) + tt) * E
        return [(st_ref[base + e] // BF16_ROWS) * BF16_ROWS for e in range(E)]

    def window_copy(bb, e, start, sl):
        row0 = pl.multiple_of((e * B + bb) * cap + start, BF16_ROWS)
        return pltpu.make_async_copy(yg_hbm.at[pl.ds(row0, W), :], ybuf.at[sl, pl.ds(e * W, W), :], sem.at[sl])

    def start_round0(bb, tt, sl):
        lo_n = tile_lo(bb, tt)
        for e in range(E):
            window_copy(bb, e, jnp.minimum(lo_n[e], cap - W), sl).start()

    @pl.when(step == 0)
    def _():
        start_round0(b, t, slot)

    @pl.when(step + 1 < B * nt)
    def _():
        start_round0((step + 1) // nt, (step + 1) % nt, 1 - slot)

    lo = tile_lo(b, t)
    base = (b * (nt + 1) + t) * E
    hi = [st_ref[base + E + e] for e in range(E)]
    nrounds = jnp.int32(1)
    for e in range(E):
        nrounds = jnp.maximum(nrounds, (hi[e] - lo[e] + (W - 1)) // W)

    qi = posq_ref[0].astype(I32)
    erow = lax.broadcasted_iota(I32, (LANE, KW), 0)
    ecol = lax.broadcasted_iota(I32, (LANE, KW), 1) // W
    expand = jnp.where(erow == ecol, 1.0, 0.0).astype(BF16)
    qe = (_dot((qi >> 5).astype(F32).astype(BF16), expand) * 32.0
          + _dot((qi & 31).astype(F32).astype(BF16), expand)).astype(I32) - 1
    lane_e = lax.broadcasted_iota(I32, (1, KW), 1) // W
    lane_k = lax.broadcasted_iota(I32, (1, KW), 1) % W

    def one_round(r, prefetched):
        lo_r = [lo[e] + r * W for e in range(E)]
        st = [jnp.minimum(lo_r[e], cap - W) for e in range(E)]
        if not prefetched:
            for e in range(E):
                window_copy(b, e, st[e], slot).start()
        stv = jnp.zeros((1, KW), I32)
        lov = jnp.zeros((1, KW), I32)
        for e in range(E):
            stv = jnp.where(lane_e == e, st[e], stv)
            lov = jnp.where(lane_e == e, lo_r[e], lov)
        onehot = (qe - stv == lane_k) & (qe >= lov) & (qe < lov + W)
        s = jnp.where(onehot, 1.0, 0.0).astype(BF16)
        for e in range(E):
            window_copy(b, e, st[e], slot).wait()
        return _dot(s, ybuf[slot])

    acc_ref[...] = one_round(0, True)

    def extra(r, carry):
        acc_ref[...] += one_round(r, False)
        return carry

    lax.fori_loop(1, nrounds, extra, 0)
    x2 = x_ref[...] + g_ref[0, pl.ds(b, 1), :] * acc_ref[...]
    o_ref[...] = x2
    ssq_ref[...] = jnp.sum(x2 * x2, axis=-1, keepdims=True)


def combine(starts_flat, x2d, mods, layer, posq, yg, cfg):
    B, N, D, E, cap = cfg.B, cfg.N, cfg.D, cfg.E, cfg.CAP
    nt = N // TOK_TILE
    assert cap >= COMB_WIN and cap % COMB_WIN == 0 and COMB_WIN % BF16_ROWS == 0 and cap <= 32 * 32
    return pl.pallas_call(
        functools.partial(_combine_kernel, E=E, cap=cap, B=B, nt=nt),
        grid_spec=pltpu.PrefetchScalarGridSpec(
            num_scalar_prefetch=1,
            grid=(B, nt),
            in_specs=[pl.BlockSpec((TOK_TILE, D), lambda b, t, st: (b * nt + t, 0)),
                      pl.BlockSpec((1, 8, D), lambda b, t, st: (layer, 0, 5)),
                      pl.BlockSpec((1, TOK_TILE, LANE), lambda b, t, st: (b, t, 0)),
                      pl.BlockSpec(memory_space=pl.ANY)],
            out_specs=[pl.BlockSpec((TOK_TILE, D), lambda b, t, st: (b * nt + t, 0)),
                       pl.BlockSpec((TOK_TILE, 1), lambda b, t, st: (b * nt + t, 0))],
            scratch_shapes=[pltpu.VMEM((2, E * COMB_WIN, D), BF16),
                            pltpu.VMEM((TOK_TILE, D), F32),
                            pltpu.SemaphoreType.DMA((2,))]),
        out_shape=[jax.ShapeDtypeStruct((B * N, D), F32),
                   jax.ShapeDtypeStruct((B * N, 1), F32)],
        compiler_params=_cp(("arbitrary", "arbitrary")),
        name="combine",
    )(starts_flat, x2d, mods, posq, yg)


def _pool_kernel(x_ref, ssq_ref, g_ref, sh_ref, sc_ref, o_ref, hs_ref, *, N, D, windows, chunks_per_group):
    b = pl.program_id(0)
    j = pl.program_id(1)
    P = max(windows) // 2
    cw = x_ref.shape[2]
    nblk = N // TOK_TILE
    g = g_ref[0]
    sh = sh_ref[0, pl.ds(b, 1), :]
    sc = sc_ref[0, pl.ds(b, 1), :]
    zeros = jnp.zeros((P, cw), F32)
    hs_ref[0:P, :] = zeros
    hs_ref[P + N:P + N + P, :] = zeros

    def fill(k, carry):
        r0 = pl.multiple_of(k * TOK_TILE, TOK_TILE)
        rstd = lax.rsqrt(ssq_ref[pl.ds(r0, TOK_TILE), :] * (1.0 / D) + NORM_EPS)
        y = x_ref[0, pl.ds(r0, TOK_TILE), :] * rstd * g
        hs_ref[pl.ds(pl.multiple_of(P + r0, 8), TOK_TILE), :] = y * (1.0 + sc) + sh
        return carry

    lax.fori_loop(0, nblk, fill, 0)

    for grp, w in enumerate(windows):
        half = w // 2

        @pl.when(j // chunks_per_group == grp)
        def _(half=half):
            def pool(k, carry):
                r0 = pl.multiple_of(k * TOK_TILE, TOK_TILE)
                big = hs_ref[pl.ds(r0, TOK_TILE + 2 * P), :]
                nrow = TOK_TILE + 2 * P
                fwd = big
                span = 1
                while span < 2 * half:
                    fwd = fwd + pltpu.roll(fwd, nrow - span, 0)
                    span *= 2
                acc = fwd[P - half:P - half + TOK_TILE]
                tk = lax.broadcasted_iota(I32, (TOK_TILE, 1), 0) + r0
                cnt = (jnp.minimum(tk + half, N) - jnp.maximum(tk - half, 0)).astype(F32)
                o_ref[0, pl.ds(r0, TOK_TILE), :] = (acc / cnt - big[P:P + TOK_TILE]).astype(o_ref.dtype)
                return carry

            lax.fori_loop(0, nblk, pool, 0)


def pool_mixer_input(x3, ssq, g3, mods, layer, cfg, cw):
    B, N, D = x3.shape
    dg = D // cfg.G
    cw = min(cw, dg)
    P = max(cfg.windows) // 2
    nj = D // cw

    def mspec(chunk):
        return pl.BlockSpec((1, 8, cw), lambda b, j: (layer, 0, chunk * nj + j))

    return pl.pallas_call(
        functools.partial(_pool_kernel, N=N, D=D, windows=cfg.windows, chunks_per_group=dg // cw),
        grid=(B, nj),
        in_specs=[pl.BlockSpec((1, N, cw), lambda b, j: (b, 0, j)),
                  pl.BlockSpec((N, 1), lambda b, j: (b, 0)),
                  pl.BlockSpec((1, 1, cw), lambda b, j: (layer, 0, j)),
                  mspec(0), mspec(1)],
        out_specs=pl.BlockSpec((1, N, cw), lambda b, j: (b, 0, j)),
        out_shape=jax.ShapeDtypeStruct((B, N, D), BF16),
        scratch_shapes=[pltpu.VMEM((N + 2 * P, cw), F32)],
        compiler_params=_cp(("parallel", "parallel")),
        name="pool",
    )(x3, ssq, g3, mods, mods)


def moe_block(x2d, mods, layer, g_ffn3, w_router, w_gate, w_up, w_down, cfg):
    B, N, D, E, cap = cfg.B, cfg.N, cfg.D, cfg.E, cfg.CAP
    nt = N // TOK_TILE
    wr_pad = jnp.pad(w_router, ((0, 0), (0, LANE - E)))
    hfx = ffn_norm_router(x2d.reshape(B, N, D), g_ffn3, mods, layer, wr_pad, E, tm=512)
    idx, posq, starts = route(hfx, cfg)
    idx_flat = jnp.transpose(idx, (1, 0, 2)).reshape(-1)
    starts_flat = jnp.transpose(starts[:, :, :nt + 1], (0, 2, 1)).reshape(-1)
    yg = experts(idx_flat, hfx.reshape(B * N, D // 2 + LANE), w_gate, w_up, w_down, layer, B * cap,
                 tf=256, tn=512)
    return combine(starts_flat, x2d, mods, layer, posq, yg, cfg)


def forward(cfg, x, c, ctx, c_ctx, ada_w, ada_b, norm_mix_g, norm_ffn_g, na_w_qkv, na_q_gain,
            na_k_gain, na_rpb, na_w_out, pool_w, pool_scale, moe_w_router, moe_w_gate,
            moe_w_up, moe_w_down):
    B, N, D, NC, H = cfg.B, cfg.N, cfg.D, cfg.NC, cfg.H
    dh = D // H
    L = ada_w.shape[0]
    assert L == 2 and B + 1 <= 8
    cond8 = jnp.concatenate([c, c_ctx[None, :], jnp.zeros((8 - B - 1, D), F32)], axis=0)
    mods = ada_mods(cond8, ada_w, ada_b)
    gmix3 = norm_mix_g.reshape(L, 1, D)
    gffn3 = norm_ffn_g.reshape(L, 1, D)

    h = normmod(x, gmix3, mods, 0, 0, None, tm=512).reshape(B * N, D)
    hc = normmod(ctx, gmix3, mods, 0, 0, B, tm=512).reshape(B * NC, D)
    scale = 1.0 / math.sqrt(dh)
    gain = jnp.concatenate([jnp.tile(na_q_gain[0] * scale, H), jnp.tile(na_k_gain[0], H),
                            jnp.ones((D,), F32)])[None, :]
    w_qkv = na_w_qkv[0]
    q = qkv_proj(h, w_qkv, gain, 0, D, dh, True, False, tm=2048, tn=256)
    kt = qkv_proj(h, w_qkv, gain, D, D, dh, True, True, tm=2048, tn=256)
    v = qkv_proj(h, w_qkv, gain, 2 * D, D, dh, False, False, tm=2048, tn=256)
    kct = qkv_proj(hc, w_qkv, gain, D, D, dh, True, True, tm=1024, tn=512)
    vc = qkv_proj(hc, w_qkv, gain, 2 * D, D, dh, False, False, tm=1024, tn=512)
    rpb_pad = jnp.pad(na_rpb[0], ((0, 0), (0, 1), (0, LANE - (2 * NA_KW - 1))))
    o = attention(q, kt, v, kct, vc, rpb_pad, cfg)
    x1 = proj_residual(o, na_w_out[0][None], x.reshape(B * N, D), mods, 0, 2, N, None, tm=2048, tn=256)
    x2, ssq = moe_block(x1, mods, 0, gffn3, moe_w_router[0], moe_w_gate, moe_w_up, moe_w_down, cfg)

    pooled = pool_mixer_input(x2.reshape(B, N, D), ssq, gmix3, mods, 1, cfg, cw=256).reshape(B * N, D)
    x3 = proj_residual(pooled, pool_w[0], x2, mods, 1, 2, N, pool_scale[0][None, :], tm=2048, tn=1024)
    x4, _ = moe_block(x3, mods, 1, gffn3, moe_w_router[1], moe_w_gate, moe_w_up, moe_w_down, cfg)
    return x4.reshape(B, N, D)


def kernel(x, c, ctx, c_ctx, ada_w, ada_b, norm_mix_g, norm_ffn_g, na_w_qkv, na_q_gain, na_k_gain, na_rpb, na_w_out, pool_w, pool_scale, moe_w_router, moe_w_gate, moe_w_up, moe_w_down):
    B, N, D = x.shape
    E, _, F = moe_w_gate.shape[1:]
    G = pool_w.shape[1]
    cfg = Cfg(B=B, N=N, D=D, NC=ctx.shape[1], H=na_rpb.shape[1], GW=64, E=E,
              CAP=max(1, 2 * N // E), F=F, G=G, windows=(2, 4, 8, 16), n_ada=6)
    return forward(cfg, x, c, ctx, c_ctx, ada_w, ada_b, norm_mix_g, norm_ffn_g, na_w_qkv, na_q_gain,
                   na_k_gain, na_rpb, na_w_out, pool_w, pool_scale, moe_w_router, moe_w_gate,
                   moe_w_up, moe_w_down)
```

```python
import functools
import math
from typing import NamedTuple

import jax
import jax.numpy as jnp
from jax import lax
from jax.experimental import pallas as pl
from jax.experimental.pallas import tpu as pltpu

F32 = jnp.float32
BF16 = jnp.bfloat16
I32 = jnp.int32
U32 = jnp.uint32

LANE = 128
NORM_EPS = 1e-6
NEG_INF = -1e30
NA_KH = 8
NA_KW = 16
ATTN_QROWS = 4
ATTN_KROWS = ATTN_QROWS + NA_KH
TOK_TILE = 256
COMB_WIN = 64
BF16_ROWS = 16
MM_CHUNK = 512
VMEM_LIMIT = 56 * 1024 * 1024


class Cfg(NamedTuple):
    B: int
    N: int
    D: int
    NC: int
    H: int
    GW: int
    E: int
    CAP: int
    F: int
    G: int
    windows: tuple
    n_ada: int


def _cp(sem, vmem=VMEM_LIMIT):
    return pltpu.CompilerParams(dimension_semantics=sem, vmem_limit_bytes=vmem)


def _dot(a, b):
    return jnp.dot(a, b, preferred_element_type=F32)


def _dot_nt(a, b):
    return lax.dot_general(a, b, (((1,), (1,)), ((), ())), preferred_element_type=F32)


def _silu(x):
    return x * (1.0 / (1.0 + jnp.exp(-x)))


def _row_chunks(tm):
    mc = min(MM_CHUNK, tm)
    return [(r, mc) for r in range(0, tm, mc)]


def _ada_kernel(cond_ref, w_ref, b_ref, o_ref, st_ref, *, nrow):
    s = _silu(cond_ref[...])
    o_ref[0] = _dot(s[0:8].astype(BF16), w_ref[...].astype(BF16)) + b_ref[...]

    @pl.when(pl.program_id(0) == 0)
    def _():
        for k in range(s.shape[1] // LANE):
            t = s[:, k * LANE:(k + 1) * LANE].T
            for r in range(nrow):
                st_ref[r, k * LANE:(k + 1) * LANE, :] = jnp.broadcast_to(t[:, r:r + 1], (LANE, LANE))


def ada_mods(cond128, ada_w, ada_b3, layer, nrow):
    L, D, ND = ada_w.shape
    tn = min(512, ND)
    return pl.pallas_call(
        functools.partial(_ada_kernel, nrow=nrow),
        grid=(ND // tn,),
        in_specs=[pl.BlockSpec((LANE, D), lambda j: (0, 0)),
                  pl.BlockSpec((None, D, tn), lambda j: (layer, 0, j)),
                  pl.BlockSpec((None, 1, tn), lambda j: (layer, 0, j))],
        out_specs=[pl.BlockSpec((1, 8, tn), lambda j: (0, 0, j)),
                   pl.BlockSpec((nrow, D, LANE), lambda j: (0, 0, 0))],
        out_shape=[jax.ShapeDtypeStruct((1, 8, ND), F32),
                   jax.ShapeDtypeStruct((nrow, D, LANE), F32)],
        compiler_params=_cp(("arbitrary",)),
        name="ada",
    )(cond128, ada_w, ada_b3)


def _norm_mod(x, g, sh, sc):
    ms = jnp.mean(x * x, axis=-1, keepdims=True)
    y = x * lax.rsqrt(ms + NORM_EPS) * g
    return y * (1.0 + sc) + sh


def _normmod_kernel(x_ref, g_ref, sh_ref, sc_ref, o_ref, *, row):
    r = pl.program_id(0) if row is None else row
    sh = sh_ref[0, pl.ds(r, 1), :]
    sc = sc_ref[0, pl.ds(r, 1), :]
    o_ref[0] = _norm_mod(x_ref[0], g_ref[0], sh, sc).astype(o_ref.dtype)


def _mod_spec(D, layer, chunk):
    return pl.BlockSpec((1, 8, D), lambda b, i: (layer, 0, chunk))


def normmod(x3, g3, mods, layer, sh_chunk, row, tm):
    B, n, D = x3.shape
    tm = min(tm, n)
    return pl.pallas_call(
        functools.partial(_normmod_kernel, row=row),
        grid=(B, n // tm),
        in_specs=[pl.BlockSpec((1, tm, D), lambda b, i: (b, i, 0)),
                  pl.BlockSpec((1, 1, D), lambda b, i: (layer, 0, 0)),
                  _mod_spec(D, layer, sh_chunk),
                  _mod_spec(D, layer, sh_chunk + 1)],
        out_specs=pl.BlockSpec((1, tm, D), lambda b, i: (b, i, 0)),
        out_shape=jax.ShapeDtypeStruct((B, n, D), BF16),
        compiler_params=_cp(("parallel", "parallel")),
        name="normmod",
    )(x3, g3, mods, mods)


def _qkv_kernel(a_ref, w_ref, gain_ref, *rest, norm, transposed, dh, side_rows):
    if side_rows:
        aw_ref, ab_ref, st_ref, o_ref, m_ref = rest
        wa = aw_ref[...]
        wcol = wa.shape[1]
        for r in range(side_rows):
            st = st_ref[r]
            for c in range(wcol // LANE):
                prod = wa[:, c * LANE:(c + 1) * LANE] * st
                m_ref[0, r:r + 1, c * LANE:(c + 1) * LANE] = (jnp.sum(prod, axis=0, keepdims=True)
                                                              + ab_ref[:, c * LANE:(c + 1) * LANE])
        m_ref[0, side_rows:8, :] = jnp.zeros((8 - side_rows, wcol), F32)
    else:
        (o_ref,) = rest
    wb = w_ref[...].astype(BF16)
    tn = wb.shape[1]
    for r0, mc in _row_chunks(a_ref.shape[0]):
        acc = _dot(a_ref[r0:r0 + mc, :], wb)
        for c in range(tn // dh):
            y = acc[:, c * dh:(c + 1) * dh]
            if norm:
                ms = jnp.mean(y * y, axis=-1, keepdims=True)
                y = y * lax.rsqrt(ms + NORM_EPS) * gain_ref[:, c * dh:(c + 1) * dh]
            if transposed:
                o_ref[c, :, r0:r0 + mc] = y.T.astype(o_ref.dtype)
            else:
                o_ref[c, r0:r0 + mc, :] = y.astype(o_ref.dtype)


def qkv_proj(a, w, gain, col0, ncols, dh, norm, transposed, tm, tn, side=None):
    M, D = a.shape
    tm = min(tm, M)
    off = col0 // tn
    ni, nj = M // tm, ncols // tn
    if transposed:
        out_spec = pl.BlockSpec((tn // dh, dh, tm), lambda i, j: (j, 0, i))
        out_shape = jax.ShapeDtypeStruct((ncols // dh, dh, M), BF16)
    else:
        out_spec = pl.BlockSpec((tn // dh, tm, dh), lambda i, j: (j, i, 0))
        out_shape = jax.ShapeDtypeStruct((ncols // dh, M, dh), BF16)
    in_specs = [pl.BlockSpec((tm, D), lambda i, j: (i, 0)),
                pl.BlockSpec((D, tn), lambda i, j: (0, j + off)),
                pl.BlockSpec((1, tn), lambda i, j: (0, j + off))]
    args = [a, w, gain]
    side_rows = 0
    if side is not None:
        ada_w, ada_b3, st, layer, part, nparts = side
        side_rows = st.shape[0]
        ND = ada_w.shape[2]
        steps = ni * nj
        assert ND % (nparts * steps) == 0
        wcol = ND // (nparts * steps)
        assert wcol % LANE == 0
        blk0 = part * steps
        in_specs += [pl.BlockSpec((None, D, wcol), lambda i, j: (layer, 0, blk0 + i * nj + j)),
                     pl.BlockSpec((None, 1, wcol), lambda i, j: (layer, 0, blk0 + i * nj + j)),
                     pl.BlockSpec(st.shape, lambda i, j: (0, 0, 0), pipeline_mode=pl.Buffered(1))]
        args += [ada_w, ada_b3, st]
        out_spec = [out_spec, pl.BlockSpec((1, 8, wcol), lambda i, j: (0, 0, i * nj + j))]
        out_shape = [out_shape, jax.ShapeDtypeStruct((1, 8, steps * wcol), F32)]
    return pl.pallas_call(
        functools.partial(_qkv_kernel, norm=norm, transposed=transposed, dh=dh, side_rows=side_rows),
        grid=(ni, nj),
        in_specs=in_specs,
        out_specs=out_spec,
        out_shape=out_shape,
        compiler_params=_cp(("parallel", "arbitrary")),
        name="qkv",
    )(*args)


def _attn_bias_variants():
    out = []
    for var in range(3):
        tab = {}
        for i in range(ATTN_QROWS):
            for j in range(ATTN_KROWS):
                if var == 0:
                    valid, dr = j < NA_KH, j - i + NA_KH - 1
                elif var == 1:
                    valid, dr = i <= j < i + NA_KH, j - i + NA_KH // 2 - 1
                else:
                    valid, dr = j >= ATTN_KROWS - NA_KH, j - i - 1
                tab[(i, j)] = (valid, dr)
        out.append(tab)
    return out


def _attn_kernel(rpb_ref, q_ref, kt_ref, v_ref, kct_ref, vc_ref, o_ref, bias_ref, sl_ref, sc_ref,
                 pl_ref, pc_ref, den_ref, *, gw, rows):
    nblk = rows // ATTN_QROWS
    qn = ATTN_QROWS * gw
    kn = ATTN_KROWS * gw

    @pl.when(pl.program_id(1) == 0)
    def _build_bias():
        qc = lax.broadcasted_iota(I32, (gw, gw), 0)
        kc = lax.broadcasted_iota(I32, (gw, gw), 1)
        cs = jnp.clip(qc - NA_KW // 2, 0, gw - NA_KW)
        col_mask = jnp.where((kc >= cs) & (kc < cs + NA_KW), 0.0, NEG_INF).astype(F32)
        tiles = []
        for dr in range(2 * NA_KH - 1):
            r = jnp.broadcast_to(rpb_ref[0, dr:dr + 1, :], (gw, LANE))
            t = pltpu.roll(r, LANE - (NA_KW - 1), 1, stride=1, stride_axis=0)
            tiles.append(t[:, :gw] + col_mask)
        neg = jnp.full((gw, gw), NEG_INF, F32)
        for var, tab in enumerate(_attn_bias_variants()):
            for (i, j), (valid, dr) in tab.items():
                bias_ref[var, i * gw:(i + 1) * gw, j * gw:(j + 1) * gw] = tiles[dr] if valid else neg

    def geom(blk):
        if isinstance(blk, int):
            r0 = blk * ATTN_QROWS
            start = min(max(r0 - NA_KH // 2, 0), rows - ATTN_KROWS)
            var = 0 if blk == 0 else (2 if blk == nblk - 1 else 1)
            return r0 * gw, start * gw, var
        r0 = blk * ATTN_QROWS
        start = jnp.clip(r0 - NA_KH // 2, 0, rows - ATTN_KROWS)
        var = jnp.where(blk == 0, 0, jnp.where(blk == nblk - 1, 2, 1))
        return pl.multiple_of(r0 * gw, qn), pl.multiple_of(start * gw, LANE), var

    def scores(blk, slot):
        q0, k0, var = geom(blk)
        q = q_ref[pl.ds(q0, qn), :]
        sl_ref[slot] = _dot(q, kt_ref[:, pl.ds(k0, kn)]) + bias_ref[var]
        sc_ref[slot] = _dot(q, kct_ref[...])

    def softmax(slot):
        s_lat = sl_ref[slot]
        s_ctx = sc_ref[slot]
        m = jnp.maximum(jnp.max(s_lat, axis=-1, keepdims=True), jnp.max(s_ctx, axis=-1, keepdims=True))
        p_lat = jnp.exp(s_lat - m)
        p_ctx = jnp.exp(s_ctx - m)
        den_ref[slot] = jnp.sum(p_lat, axis=-1, keepdims=True) + jnp.sum(p_ctx, axis=-1, keepdims=True)
        pl_ref[slot] = p_lat.astype(BF16)
        pc_ref[slot] = p_ctx.astype(BF16)

    def values(blk, slot):
        q0, k0, _ = geom(blk)
        o = _dot(pl_ref[slot], v_ref[pl.ds(k0, kn), :]) + _dot(pc_ref[slot], vc_ref[...])
        o_ref[pl.ds(q0, qn), :] = (o * (1.0 / den_ref[slot])).astype(o_ref.dtype)

    scores(0, 0)
    scores(1, 1)
    softmax(0)

    def body(it, carry):
        i = 2 + 2 * it
        scores(i, 0)
        softmax(1)
        values(i - 2, 0)
        scores(i + 1, 1)
        softmax(0)
        values(i - 1, 1)
        return carry

    lax.fori_loop(0, (nblk - 2) // 2, body, 0)
    softmax(1)
    values(nblk - 2, 0)
    values(nblk - 1, 1)


def attention(q, kt, v, kct, vc, rpb_pad, cfg):
    B, N, D, NC, H, GW = cfg.B, cfg.N, cfg.D, cfg.NC, cfg.H, cfg.GW
    dh = D // H
    rows = N // GW
    assert dh == LANE and rows >= ATTN_KROWS and rows % (2 * ATTN_QROWS) == 0 and GW >= NA_KW
    assert (ATTN_QROWS * GW) % LANE == 0 and (NA_KH // 2 * GW) % LANE == 0 and ((rows - ATTN_KROWS) * GW) % LANE == 0
    return pl.pallas_call(
        functools.partial(_attn_kernel, gw=GW, rows=rows),
        grid=(H, B),
        in_specs=[pl.BlockSpec((1, 2 * NA_KH, LANE), lambda h, b: (h, 0, 0)),
                  pl.BlockSpec((None, N, dh), lambda h, b: (h, b, 0)),
                  pl.BlockSpec((None, dh, N), lambda h, b: (h, 0, b)),
                  pl.BlockSpec((None, N, dh), lambda h, b: (h, b, 0)),
                  pl.BlockSpec((None, dh, NC), lambda h, b: (h, 0, b)),
                  pl.BlockSpec((None, NC, dh), lambda h, b: (h, b, 0))],
        out_specs=pl.BlockSpec((N, dh), lambda h, b: (b, h)),
        out_shape=jax.ShapeDtypeStruct((B * N, D), BF16),
        scratch_shapes=[pltpu.VMEM((3, ATTN_QROWS * GW, ATTN_KROWS * GW), F32),
                        pltpu.VMEM((2, ATTN_QROWS * GW, ATTN_KROWS * GW), F32),
                        pltpu.VMEM((2, ATTN_QROWS * GW, NC), F32),
                        pltpu.VMEM((2, ATTN_QROWS * GW, ATTN_KROWS * GW), BF16),
                        pltpu.VMEM((2, ATTN_QROWS * GW, NC), BF16),
                        pltpu.VMEM((2, ATTN_QROWS * GW, 1), F32)],
        compiler_params=_cp(("arbitrary", "arbitrary")),
        name="attn",
    )(rpb_pad, q, kt, v, kct, vc)


def _proj_res_kernel(a_ref, w_ref, x_ref, g_ref, *rest, tiles_per_batch, scaled):
    if scaled:
        ps_ref, o_ref = rest
    else:
        (o_ref,) = rest
    wb = w_ref[0].astype(BF16)
    b = pl.program_id(0) // tiles_per_batch
    g = g_ref[0, pl.ds(b, 1), :]
    for r0, mc in _row_chunks(a_ref.shape[0]):
        y = _dot(a_ref[r0:r0 + mc, :], wb)
        if scaled:
            y = y * ps_ref[...]
        o_ref[r0:r0 + mc, :] = x_ref[r0:r0 + mc, :] + g * y


def proj_residual(a, w3, x2, mods, layer, gate_chunk, n_per_batch, pscale, tm, tn):
    M = a.shape[0]
    G, K, KO = w3.shape
    Dout = G * KO
    tm = min(tm, n_per_batch)
    tn = min(tn, KO)
    nj = KO // tn
    scaled = pscale is not None
    in_specs = [pl.BlockSpec((tm, K), lambda i, g, j: (i, g)),
                pl.BlockSpec((1, K, tn), lambda i, g, j: (g, 0, j)),
                pl.BlockSpec((tm, tn), lambda i, g, j: (i, g * nj + j)),
                pl.BlockSpec((1, 8, tn), lambda i, g, j: (layer, 0, gate_chunk * (Dout // tn) + g * nj + j))]
    args = [a, w3, x2, mods]
    if scaled:
        in_specs.append(pl.BlockSpec((1, tn), lambda i, g, j: (0, g * nj + j)))
        args.append(pscale)
    return pl.pallas_call(
        functools.partial(_proj_res_kernel, tiles_per_batch=n_per_batch // tm, scaled=scaled),
        grid=(M // tm, G, nj),
        in_specs=in_specs,
        out_specs=pl.BlockSpec((tm, tn), lambda i, g, j: (i, g * nj + j)),
        out_shape=jax.ShapeDtypeStruct((M, Dout), F32),
        compiler_params=_cp(("parallel", "arbitrary", "arbitrary")),
        name="proj_res",
    )(*args)


def _split_bf16(x):
    hi = x.astype(BF16)
    lo = (x - hi.astype(F32)).astype(BF16)
    return hi, lo


def _ffn_norm_kernel(x_ref, g_ref, sh_ref, sc_ref, wr_ref, hfx_ref, *, E):
    b = pl.program_id(0)
    D = x_ref.shape[2]
    hf = _norm_mod(x_ref[0], g_ref[0], sh_ref[0, pl.ds(b, 1), :], sc_ref[0, pl.ds(b, 1), :])
    bits = pltpu.bitcast(hf.astype(BF16).astype(F32), U32)
    hfx_ref[0, :, 0:D // 2] = (bits[:, 0:D // 2] >> 16) | bits[:, D // 2:D]
    h_hi, h_lo = _split_bf16(hf)
    w_hi, w_lo = _split_bf16(wr_ref[...])
    r = _dot(h_hi, jnp.concatenate([w_hi, w_lo], axis=1))
    lg = r[:, 0:LANE] + (r[:, LANE:2 * LANE] + _dot(h_lo, w_hi))
    lane = lax.broadcasted_iota(I32, (1, LANE), 1)
    lg = jnp.where(lane < E, lg, NEG_INF)
    ex = jnp.exp(lg - jnp.max(lg, axis=-1, keepdims=True))
    aff = ex / jnp.sum(ex, axis=-1, keepdims=True)
    hfx_ref[0, :, D // 2:D // 2 + LANE] = pltpu.bitcast(aff, U32)


def ffn_norm_router(x3, g3, mods, layer, wr_pad, E, tm):
    B, N, D = x3.shape
    tm = min(tm, N)
    return pl.pallas_call(
        functools.partial(_ffn_norm_kernel, E=E),
        grid=(B, N // tm),
        in_specs=[pl.BlockSpec((1, tm, D), lambda b, i: (b, i, 0)),
                  pl.BlockSpec((1, 1, D), lambda b, i: (layer, 0, 0)),
                  _mod_spec(D, layer, 3),
                  _mod_spec(D, layer, 4),
                  pl.BlockSpec((D, LANE), lambda b, i: (0, 0))],
        out_specs=pl.BlockSpec((1, tm, D // 2 + LANE), lambda b, i: (b, i, 0)),
        out_shape=jax.ShapeDtypeStruct((B, N, D // 2 + LANE), U32),
        compiler_params=_cp(("parallel", "parallel")),
        name="ffn_norm",
    )(x3, g3, mods, mods, wr_pad)


def _route_kernel(aff_ref, idx_ref, posq_ref, starts_ref, t_ref, cum_ref, *, E, cap, N):
    b = pl.program_id(0)
    nblk = N // TOK_TILE
    logn = int(math.log2(N))
    for k in range(N // LANE):
        t_ref[:, k * LANE:(k + 1) * LANE] = pltpu.bitcast(aff_ref[0, k * LANE:(k + 1) * LANE, :], F32).T
    bits = pltpu.bitcast(t_ref[0:E, :], I32)
    tok = lax.broadcasted_iota(I32, (1, N), 1)

    def count(pred):
        return jnp.sum(jnp.where(pred, 1, 0), axis=1, keepdims=True)

    def thr_step(i, thr):
        cand = thr | (jnp.int32(1) << (30 - i))
        return jnp.where(count(bits >= cand) >= cap, cand, thr)

    thr = lax.fori_loop(0, 31, thr_step, jnp.zeros((E, 1), I32))
    gt = bits > thr
    eq = bits == thr
    need = cap - count(gt)

    def tie_step(i, ans):
        cand = ans | (jnp.int32(1) << (logn - 1 - i))
        return jnp.where(count(eq & (tok < cand)) < need, cand, ans)

    last = lax.fori_loop(0, logn, tie_step, jnp.zeros((E, 1), I32))
    sel = gt | (eq & (tok <= last))
    sel_f = jnp.where(sel, 1.0, 0.0).astype(F32)

    ri = lax.broadcasted_iota(I32, (TOK_TILE, TOK_TILE), 0)
    ci = lax.broadcasted_iota(I32, (TOK_TILE, TOK_TILE), 1)
    tri = jnp.where(ri <= ci, 1.0, 0.0).astype(BF16)
    lane_s = lax.broadcasted_iota(I32, (1, LANE), 1)
    carry = jnp.zeros((E, 1), F32)
    starts = jnp.zeros((E, LANE), F32)
    for k in range(nblk):
        m = sel_f[:, k * TOK_TILE:(k + 1) * TOK_TILE]
        starts = jnp.where(lane_s == k, carry, starts)
        cum_ref[:, k * TOK_TILE:(k + 1) * TOK_TILE] = _dot(m.astype(BF16), tri) + carry
        carry = carry + jnp.sum(m, axis=1, keepdims=True)
    starts_ref[0] = jnp.where(lane_s == nblk, carry, starts).astype(I32)

    lane_c = lax.broadcasted_iota(I32, (1, cap), 1)

    def slot_step(c, out):
        cnt = jnp.sum(jnp.where(cum_ref[...] <= lax.convert_element_type(c, F32), 1, 0), axis=1, keepdims=True)
        return jnp.where(lane_c == c, cnt, out)

    idx = lax.fori_loop(0, cap, slot_step, jnp.zeros((E, cap), I32), unroll=4)
    idx_ref[0] = idx + b * N

    t_ref[0:E, :] = jnp.where(sel, cum_ref[...], 0.0)
    t_ref[E:LANE, :] = jnp.zeros((LANE - E, N), F32)
    for k in range(N // LANE):
        posq_ref[0, k * LANE:(k + 1) * LANE, :] = t_ref[:, k * LANE:(k + 1) * LANE].T


def route(hfx, cfg):
    B, N, D, E, cap = cfg.B, cfg.N, cfg.D, cfg.E, cfg.CAP
    nblk = N // TOK_TILE
    assert N % TOK_TILE == 0 and cap % LANE == 0 and (1 << int(math.log2(N))) == N
    assert E % 8 == 0 and nblk < LANE
    return pl.pallas_call(
        functools.partial(_route_kernel, E=E, cap=cap, N=N),
        grid=(B,),
        in_specs=[pl.BlockSpec((1, N, LANE), lambda b: (b, 0, D // 2 // LANE))],
        out_specs=[pl.BlockSpec((1, E, cap), lambda b: (b, 0, 0)),
                   pl.BlockSpec((1, N, LANE), lambda b: (b, 0, 0)),
                   pl.BlockSpec((1, E, LANE), lambda b: (b, 0, 0))],
        out_shape=[jax.ShapeDtypeStruct((B, E, cap), I32),
                   jax.ShapeDtypeStruct((B, N, LANE), F32),
                   jax.ShapeDtypeStruct((B, E, LANE), I32)],
        scratch_shapes=[pltpu.VMEM((LANE, N), F32), pltpu.VMEM((E, N), F32)],
        compiler_params=_cp(("arbitrary",)),
        name="route",
    )(hfx)


def _expert_kernel(idx_ref, hfx_hbm, wg_ref, wu_ref, wd_ref, o_ref, xs_ref, ga_ref, stg_ref, hid_ref,
                   sem, *, nf, tf, nn):
    e = pl.program_id(0)
    s = pl.program_id(1)
    ne = pl.num_programs(0)
    rows, D = xs_ref.shape
    rpt = rows // nn

    def row_copy(expert, r):
        tok = idx_ref[expert * rows + r]
        return pltpu.make_async_copy(hfx_hbm.at[pl.ds(tok, 1), :], stg_ref.at[pl.ds(r, 1), :], sem.at[0])

    @pl.when((e == 0) & (s == 0))
    def _():
        def body(r, carry):
            row_copy(0, r).start()
            return carry
        lax.fori_loop(0, rows, body, 0, unroll=8)

    @pl.when(s == 0)
    def _():
        pltpu.make_async_copy(hfx_hbm.at[pl.ds(0, rows), :], stg_ref, sem.at[0]).wait()
        for r0, mc in _row_chunks(rows):
            w = stg_ref[r0:r0 + mc, 0:D // 2]
            xs_ref[r0:r0 + mc, 0:D // 2] = pltpu.bitcast(w << 16, F32).astype(BF16)
            xs_ref[r0:r0 + mc, D // 2:D] = pltpu.bitcast(w & jnp.uint32(0xFFFF0000), F32).astype(BF16)
        ga_ref[...] = pltpu.bitcast(stg_ref[:, D // 2:D // 2 + LANE], F32)

    @pl.when(s < nf)
    def _():
        wg = wg_ref[0].astype(BF16)
        wu = wu_ref[0].astype(BF16)
        for r0, mc in _row_chunks(rows):
            x = xs_ref[r0:r0 + mc, :]
            hid_ref[s, r0:r0 + mc, :] = (_silu(_dot(x, wg)) * _dot(x, wu)).astype(BF16)

    @pl.when(s >= nf)
    def _():
        nxt = jnp.where(e + 1 < ne, e + 1, 0)
        base = (s - nf) * rpt
        for j in range(rpt):
            row_copy(nxt, base + j).start()
        wd = wd_ref[0].astype(BF16)
        lane = lax.broadcasted_iota(I32, (1, LANE), 1)
        for r0, mc in _row_chunks(rows):
            acc = _dot(hid_ref[0, r0:r0 + mc, :], wd[0:tf, :])
            for k in range(1, nf):
                acc = acc + _dot(hid_ref[k, r0:r0 + mc, :], wd[k * tf:(k + 1) * tf, :])
            gate = jnp.sum(jnp.where(lane == e, ga_ref[r0:r0 + mc, :], 0.0), axis=-1, keepdims=True)
            o_ref[r0:r0 + mc, :] = (acc * gate).astype(o_ref.dtype)

    @pl.when((e == ne - 1) & (s == pl.num_programs(1) - 1))
    def _():
        pltpu.make_async_copy(hfx_hbm.at[pl.ds(0, rows), :], stg_ref, sem.at[0]).wait()


def experts(idx_flat, hfx2, w_gate, w_up, w_down, layer, rows, tf, tn):
    _, E, D, F = w_gate.shape
    M = idx_flat.shape[0]
    DX = hfx2.shape[1]
    assert M == E * rows and DX == D // 2 + LANE
    tf = min(tf, F)
    tn = min(tn, D)
    nf, nn = F // tf, D // tn
    assert rows % nn == 0

    def w_in(e, s, idx):
        return (layer, e, 0, jnp.minimum(s, nf - 1))

    return pl.pallas_call(
        functools.partial(_expert_kernel, nf=nf, tf=tf, nn=nn),
        grid_spec=pltpu.PrefetchScalarGridSpec(
            num_scalar_prefetch=1,
            grid=(E, nf + nn),
            in_specs=[pl.BlockSpec(memory_space=pl.ANY),
                      pl.BlockSpec((None, 1, D, tf), w_in),
                      pl.BlockSpec((None, 1, D, tf), w_in),
                      pl.BlockSpec((None, 1, F, tn), lambda e, s, idx: (layer, e, 0, jnp.maximum(s - nf, 0)))],
            out_specs=pl.BlockSpec((rows, tn), lambda e, s, idx: (e, jnp.maximum(s - nf, 0))),
            scratch_shapes=[pltpu.VMEM((rows, D), BF16),
                            pltpu.VMEM((rows, LANE), F32),
                            pltpu.VMEM((rows, DX), U32),
                            pltpu.VMEM((nf, rows, tf), BF16),
                            pltpu.SemaphoreType.DMA((1,))]),
        out_shape=jax.ShapeDtypeStruct((M, D), BF16),
        compiler_params=_cp(("arbitrary", "arbitrary")),
        name="experts",
    )(idx_flat, hfx2, w_gate, w_up, w_down)


def _combine_kernel(st_ref, x_ref, g_ref, posq_ref, yg_hbm, o_ref, ssq_ref, ybuf, acc_ref, sem,
                    *, E, cap, B, nt):
    b = pl.program_id(0)
    t = pl.program_id(1)
    W = COMB_WIN
    KW = E * W
    step = b * nt + t
    slot = step % 2

    def tile_lo(bb, tt):
        base = (bb * (nt + 1) + tt) * E
        return [(st_ref[base + e] // BF16_ROWS) * BF16_ROWS for e in range(E)]

    def window_copy(bb, e, start, sl):
        row0 = pl.multiple_of((e * B + bb) * cap + start, BF16_ROWS)
        return pltpu.make_async_copy(yg_hbm.at[pl.ds(row0, W), :], ybuf.at[sl, pl.ds(e * W, W), :], sem.at[sl])

    def start_round0(bb, tt, sl):
        lo_n = tile_lo(bb, tt)
        for e in range(E):
            window_copy(bb, e, jnp.minimum(lo_n[e], cap - W), sl).start()

    @pl.when(step == 0)
    def _():
        start_round0(b, t, slot)

    @pl.when(step + 1 < B * nt)
    def _():
        start_round0((step + 1) // nt, (step + 1) % nt, 1 - slot)

    lo = tile_lo(b, t)
    base = (b * (nt + 1) + t) * E
    hi = [st_ref[base + E + e] for e in range(E)]
    nrounds = jnp.int32(1)
    for e in range(E):
        nrounds = jnp.maximum(nrounds, (hi[e] - lo[e] + (W - 1)) // W)

    qi = posq_ref[0].astype(I32)
    erow = lax.broadcasted_iota(I32, (LANE, KW), 0)
    ecol = lax.broadcasted_iota(I32, (LANE, KW), 1) // W
    expand = jnp.where(erow == ecol, 1.0, 0.0).astype(BF16)
    qe = (_dot((qi >> 5).astype(F32).astype(BF16), expand) * 32.0
          + _dot((qi & 31).astype(F32).astype(BF16), expand)).astype(I32) - 1
    lane_e = lax.broadcasted_iota(I32, (1, KW), 1) // W
    lane_k = lax.broadcasted_iota(I32, (1, KW), 1) % W

    def one_round(r, prefetched):
        lo_r = [lo[e] + r * W for e in range(E)]
        st = [jnp.minimum(lo_r[e], cap - W) for e in range(E)]
        if not prefetched:
            for e in range(E):
                window_copy(b, e, st[e], slot).start()
        stv = jnp.zeros((1, KW), I32)
        lov = jnp.zeros((1, KW), I32)
        for e in range(E):
            stv = jnp.where(lane_e == e, st[e], stv)
            lov = jnp.where(lane_e == e, lo_r[e], lov)
        onehot = (qe - stv == lane_k) & (qe >= lov) & (qe < lov + W)
        s = jnp.where(onehot, 1.0, 0.0).astype(BF16)
        for e in range(E):
            window_copy(b, e, st[e], slot).wait()
        return _dot(s, ybuf[slot])

    acc_ref[...] = one_round(0, True)

    def extra(r, carry):
        acc_ref[...] += one_round(r, False)
        return carry

    lax.fori_loop(1, nrounds, extra, 0)
    x2 = x_ref[...] + g_ref[0, pl.ds(b, 1), :] * acc_ref[...]
    o_ref[...] = x2
    ssq_ref[...] = jnp.sum(x2 * x2, axis=-1, keepdims=True)


def combine(starts_flat, x2d, mods, layer, posq, yg, cfg):
    B, N, D, E, cap = cfg.B, cfg.N, cfg.D, cfg.E, cfg.CAP
    nt = N // TOK_TILE
    assert cap >= COMB_WIN and cap % COMB_WIN == 0 and COMB_WIN % BF16_ROWS == 0 and cap <= 32 * 32
    return pl.pallas_call(
        functools.partial(_combine_kernel, E=E, cap=cap, B=B, nt=nt),
        grid_spec=pltpu.PrefetchScalarGridSpec(
            num_scalar_prefetch=1,
            grid=(B, nt),
            in_specs=[pl.BlockSpec((TOK_TILE, D), lambda b, t, st: (b * nt + t, 0)),
                      pl.BlockSpec((1, 8, D), lambda b, t, st: (layer, 0, 5)),
                      pl.BlockSpec((1, TOK_TILE, LANE), lambda b, t, st: (b, t, 0)),
                      pl.BlockSpec(memory_space=pl.ANY)],
            out_specs=[pl.BlockSpec((TOK_TILE, D), lambda b, t, st: (b * nt + t, 0)),
                       pl.BlockSpec((TOK_TILE, 1), lambda b, t, st: (b * nt + t, 0))],
            scratch_shapes=[pltpu.VMEM((2, E * COMB_WIN, D), BF16),
                            pltpu.VMEM((TOK_TILE, D), F32),
                            pltpu.SemaphoreType.DMA((2,))]),
        out_shape=[jax.ShapeDtypeStruct((B * N, D), F32),
                   jax.ShapeDtypeStruct((B * N, 1), F32)],
        compiler_params=_cp(("arbitrary", "arbitrary")),
        name="combine",
    )(starts_flat, x2d, mods, posq, yg)


def _pool_kernel(x_ref, ssq_ref, g_ref, sh_ref, sc_ref, o_ref, hs_ref, *, N, D, windows, chunks_per_group):
    b = pl.program_id(0)
    j = pl.program_id(1)
    P = max(windows) // 2
    cw = x_ref.shape[2]
    nblk = N // TOK_TILE
    g = g_ref[0]
    sh = sh_ref[0, pl.ds(b, 1), :]
    sc = sc_ref[0, pl.ds(b, 1), :]
    zeros = jnp.zeros((P, cw), F32)
    hs_ref[0:P, :] = zeros
    hs_ref[P + N:P + N + P, :] = zeros

    def fill(k, carry):
        r0 = pl.multiple_of(k * TOK_TILE, TOK_TILE)
        rstd = lax.rsqrt(ssq_ref[pl.ds(r0, TOK_TILE), :] * (1.0 / D) + NORM_EPS)
        y = x_ref[0, pl.ds(r0, TOK_TILE), :] * rstd * g
        hs_ref[pl.ds(pl.multiple_of(P + r0, 8), TOK_TILE), :] = y * (1.0 + sc) + sh
        return carry

    lax.fori_loop(0, nblk, fill, 0)

    for grp, w in enumerate(windows):
        half = w // 2

        @pl.when(j // chunks_per_group == grp)
        def _(half=half):
            def pool(k, carry):
                r0 = pl.multiple_of(k * TOK_TILE, TOK_TILE)
                big = hs_ref[pl.ds(r0, TOK_TILE + 2 * P), :]
                nrow = TOK_TILE + 2 * P
                fwd = big
                span = 1
                while span < 2 * half:
                    fwd = fwd + pltpu.roll(fwd, nrow - span, 0)
                    span *= 2
                acc = fwd[P - half:P - half + TOK_TILE]
                tk = lax.broadcasted_iota(I32, (TOK_TILE, 1), 0) + r0
                cnt = (jnp.minimum(tk + half, N) - jnp.maximum(tk - half, 0)).astype(F32)
                o_ref[0, pl.ds(r0, TOK_TILE), :] = (acc / cnt - big[P:P + TOK_TILE]).astype(o_ref.dtype)
                return carry

            lax.fori_loop(0, nblk, pool, 0)


def pool_mixer_input(x3, ssq, g3, mods, layer, cfg, cw):
    B, N, D = x3.shape
    dg = D // cfg.G
    cw = min(cw, dg)
    P = max(cfg.windows) // 2
    nj = D // cw

    def mspec(chunk):
        return pl.BlockSpec((1, 8, cw), lambda b, j: (layer, 0, chunk * nj + j))

    return pl.pallas_call(
        functools.partial(_pool_kernel, N=N, D=D, windows=cfg.windows, chunks_per_group=dg // cw),
        grid=(B, nj),
        in_specs=[pl.BlockSpec((1, N, cw), lambda b, j: (b, 0, j)),
                  pl.BlockSpec((N, 1), lambda b, j: (b, 0)),
                  pl.BlockSpec((1, 1, cw), lambda b, j: (layer, 0, j)),
                  mspec(0), mspec(1)],
        out_specs=pl.BlockSpec((1, N, cw), lambda b, j: (b, 0, j)),
        out_shape=jax.ShapeDtypeStruct((B, N, D), BF16),
        scratch_shapes=[pltpu.VMEM((N + 2 * P, cw), F32)],
        compiler_params=_cp(("parallel", "parallel")),
        name="pool",
    )(x3, ssq, g3, mods, mods)


def moe_block(x2d, mods, layer, g_ffn3, w_router, w_gate, w_up, w_down, cfg):
    B, N, D, E, cap = cfg.B, cfg.N, cfg.D, cfg.E, cfg.CAP
    nt = N // TOK_TILE
    wr_pad = jnp.pad(w_router, ((0, 0), (0, LANE - E)))
    hfx = ffn_norm_router(x2d.reshape(B, N, D), g_ffn3, mods, layer, wr_pad, E, tm=512)
    idx, posq, starts = route(hfx, cfg)
    idx_flat = jnp.transpose(idx, (1, 0, 2)).reshape(-1)
    starts_flat = jnp.transpose(starts[:, :, :nt + 1], (0, 2, 1)).reshape(-1)
    yg = experts(idx_flat, hfx.reshape(B * N, D // 2 + LANE), w_gate, w_up, w_down, layer, B * cap,
                 tf=256, tn=512)
    return combine(starts_flat, x2d, mods, layer, posq, yg, cfg)


def forward(cfg, x, c, ctx, c_ctx, ada_w, ada_b, norm_mix_g, norm_ffn_g, na_w_qkv, na_q_gain,
            na_k_gain, na_rpb, na_w_out, pool_w, pool_scale, moe_w_router, moe_w_gate,
            moe_w_up, moe_w_down):
    B, N, D, NC, H = cfg.B, cfg.N, cfg.D, cfg.NC, cfg.H
    dh = D // H
    L = ada_w.shape[0]
    assert L == 2 and B + 1 <= 8
    cond128 = jnp.concatenate([c, c_ctx[None, :], jnp.zeros((LANE - B - 1, D), F32)], axis=0)
    ada_b3 = ada_b.reshape(L, 1, ada_b.shape[1])
    mods0, st = ada_mods(cond128, ada_w, ada_b3, 0, B)
    gmix3 = norm_mix_g.reshape(L, 1, D)
    gffn3 = norm_ffn_g.reshape(L, 1, D)

    h = normmod(x, gmix3, mods0, 0, 0, None, tm=512).reshape(B * N, D)
    hc = normmod(ctx, gmix3, mods0, 0, 0, B, tm=512).reshape(B * NC, D)
    scale = 1.0 / math.sqrt(dh)
    gain = jnp.concatenate([jnp.tile(na_q_gain[0] * scale, H), jnp.tile(na_k_gain[0], H),
                            jnp.ones((D,), F32)])[None, :]
    w_qkv = na_w_qkv[0]
    q, m1a = qkv_proj(h, w_qkv, gain, 0, D, dh, True, False, tm=2048, tn=256, side=(ada_w, ada_b3, st, 1, 0, 3))
    kt, m1b = qkv_proj(h, w_qkv, gain, D, D, dh, True, True, tm=2048, tn=256, side=(ada_w, ada_b3, st, 1, 1, 3))
    v, m1c = qkv_proj(h, w_qkv, gain, 2 * D, D, dh, False, False, tm=2048, tn=256, side=(ada_w, ada_b3, st, 1, 2, 3))
    mods = jnp.concatenate([mods0, jnp.concatenate([m1a, m1b, m1c], axis=2)], axis=0)
    kct = qkv_proj(hc, w_qkv, gain, D, D, dh, True, True, tm=1024, tn=512)
    vc = qkv_proj(hc, w_qkv, gain, 2 * D, D, dh, False, False, tm=1024, tn=512)
    rpb_pad = jnp.pad(na_rpb[0], ((0, 0), (0, 1), (0, LANE - (2 * NA_KW - 1))))
    o = attention(q, kt, v, kct, vc, rpb_pad, cfg)
    x1 = proj_residual(o, na_w_out[0][None], x.reshape(B * N, D), mods, 0, 2, N, None, tm=2048, tn=256)
    x2, ssq = moe_block(x1, mods, 0, gffn3, moe_w_router[0], moe_w_gate, moe_w_up, moe_w_down, cfg)

    pooled = pool_mixer_input(x2.reshape(B, N, D), ssq, gmix3, mods, 1, cfg, cw=256).reshape(B * N, D)
    x3 = proj_residual(pooled, pool_w[0], x2, mods, 1, 2, N, pool_scale[0][None, :], tm=2048, tn=1024)
    x4, _ = moe_block(x3, mods, 1, gffn3, moe_w_router[1], moe_w_gate, moe_w_up, moe_w_down, cfg)
    return x4.reshape(B, N, D)


def kernel(x, c, ctx, c_ctx, ada_w, ada_b, norm_mix_g, norm_ffn_g, na_w_qkv, na_q_gain, na_k_gain, na_rpb, na_w_out, pool_w, pool_scale, moe_w_router, moe_w_gate, moe_w_up, moe_w_down):
    B, N, D = x.shape
    E, _, F = moe_w_gate.shape[1:]
    G = pool_w.shape[1]
    cfg = Cfg(B=B, N=N, D=D, NC=ctx.shape[1], H=na_rpb.shape[1], GW=64, E=E,
              CAP=max(1, 2 * N // E), F=F, G=G, windows=(2, 4, 8, 16), n_ada=6)
    return forward(cfg, x, c, ctx, c_ctx, ada_w, ada_b, norm_mix_g, norm_ffn_g, na_w_qkv, na_q_gain,
                   na_k_gain, na_rpb, na_w_out, pool_w, pool_scale, moe_w_router, moe_w_gate,
                   moe_w_up, moe_w_down)
```

```python
import functools
import math
from typing import NamedTuple

import jax
import jax.numpy as jnp
from jax import lax
from jax.experimental import pallas as pl
from jax.experimental.pallas import tpu as pltpu

F32 = jnp.float32
BF16 = jnp.bfloat16
I32 = jnp.int32
U32 = jnp.uint32

LANE = 128
NORM_EPS = 1e-6
NEG_INF = -1e30
NA_KH = 8
NA_KW = 16
ATTN_QROWS = 4
ATTN_KROWS = ATTN_QROWS + NA_KH
TOK_TILE = 256
COMB_WIN = 64
BF16_ROWS = 16
MM_CHUNK = 512
VMEM_LIMIT = 56 * 1024 * 1024


class Cfg(NamedTuple):
    B: int
    N: int
    D: int
    NC: int
    H: int
    GW: int
    E: int
    CAP: int
    F: int
    G: int
    windows: tuple


def _cp(sem, vmem=VMEM_LIMIT):
    return pltpu.CompilerParams(dimension_semantics=sem, vmem_limit_bytes=vmem)


def _dot(a, b):
    return jnp.dot(a, b, preferred_element_type=F32)


def _dot_nt(a, b):
    return lax.dot_general(a, b, (((1,), (1,)), ((), ())), preferred_element_type=F32)


def _silu(x):
    return x * (1.0 / (1.0 + jnp.exp(-x)))


def _row_chunks(tm):
    mc = min(MM_CHUNK, tm)
    return [(r, mc) for r in range(0, tm, mc)]


def _ada_kernel(cond_ref, w_ref, b_ref, o_ref, st_ref, *, nrow):
    s = _silu(cond_ref[...])
    o_ref[0] = _dot(s[0:8].astype(BF16), w_ref[...].astype(BF16)) + b_ref[...]

    @pl.when(pl.program_id(0) == 0)
    def _():
        for k in range(s.shape[1] // LANE):
            t = s[:, k * LANE:(k + 1) * LANE].T
            for r in range(nrow):
                st_ref[r, k * LANE:(k + 1) * LANE, :] = jnp.broadcast_to(t[:, r:r + 1], (LANE, LANE))


def ada_mods(cond128, ada_w, ada_b3, layer, nrow):
    L, D, ND = ada_w.shape
    tn = min(512, ND)
    return pl.pallas_call(
        functools.partial(_ada_kernel, nrow=nrow),
        grid=(ND // tn,),
        in_specs=[pl.BlockSpec((LANE, D), lambda j: (0, 0)),
                  pl.BlockSpec((None, D, tn), lambda j: (layer, 0, j)),
                  pl.BlockSpec((None, 1, tn), lambda j: (layer, 0, j))],
        out_specs=[pl.BlockSpec((1, 8, tn), lambda j: (0, 0, j)),
                   pl.BlockSpec((nrow, D, LANE), lambda j: (0, 0, 0))],
        out_shape=[jax.ShapeDtypeStruct((1, 8, ND), F32),
                   jax.ShapeDtypeStruct((nrow, D, LANE), F32)],
        compiler_params=_cp(("arbitrary",)),
        name="ada",
    )(cond128, ada_w, ada_b3)


def _norm_mod(x, g, sh, sc):
    ms = jnp.mean(x * x, axis=-1, keepdims=True)
    y = x * lax.rsqrt(ms + NORM_EPS) * g
    return y * (1.0 + sc) + sh


def _normmod_kernel(x_ref, g_ref, sh_ref, sc_ref, o_ref, *, row):
    r = pl.program_id(0) if row is None else row
    sh = sh_ref[0, pl.ds(r, 1), :]
    sc = sc_ref[0, pl.ds(r, 1), :]
    o_ref[0] = _norm_mod(x_ref[0], g_ref[0], sh, sc).astype(o_ref.dtype)


def _mod_spec(D, layer, chunk):
    return pl.BlockSpec((1, 8, D), lambda b, i: (layer, 0, chunk))


def normmod(x3, g3, mods, layer, sh_chunk, row, tm):
    B, n, D = x3.shape
    tm = min(tm, n)
    return pl.pallas_call(
        functools.partial(_normmod_kernel, row=row),
        grid=(B, n // tm),
        in_specs=[pl.BlockSpec((1, tm, D), lambda b, i: (b, i, 0)),
                  pl.BlockSpec((1, 1, D), lambda b, i: (layer, 0, 0)),
                  _mod_spec(D, layer, sh_chunk),
                  _mod_spec(D, layer, sh_chunk + 1)],
        out_specs=pl.BlockSpec((1, tm, D), lambda b, i: (b, i, 0)),
        out_shape=jax.ShapeDtypeStruct((B, n, D), BF16),
        compiler_params=_cp(("parallel", "parallel")),
        name="normmod",
    )(x3, g3, mods, mods)


def _qkv_kernel(a_ref, w_ref, gain_ref, *rest, norm, transposed, dh, side_rows):
    if side_rows:
        aw_ref, ab_ref, st_ref, o_ref, m_ref = rest
        wa = aw_ref[...]
        wcol = wa.shape[1]
        for r in range(side_rows):
            st = st_ref[r]
            for c in range(wcol // LANE):
                prod = wa[:, c * LANE:(c + 1) * LANE] * st
                m_ref[0, r:r + 1, c * LANE:(c + 1) * LANE] = (jnp.sum(prod, axis=0, keepdims=True)
                                                              + ab_ref[:, c * LANE:(c + 1) * LANE])
        m_ref[0, side_rows:8, :] = jnp.zeros((8 - side_rows, wcol), F32)
    else:
        (o_ref,) = rest
    wb = w_ref[...].astype(BF16)
    tn = wb.shape[1]
    for r0, mc in _row_chunks(a_ref.shape[0]):
        acc = _dot(a_ref[r0:r0 + mc, :], wb)
        for c in range(tn // dh):
            y = acc[:, c * dh:(c + 1) * dh]
            if norm:
                ms = jnp.mean(y * y, axis=-1, keepdims=True)
                y = y * lax.rsqrt(ms + NORM_EPS) * gain_ref[:, c * dh:(c + 1) * dh]
            if transposed:
                o_ref[c, :, r0:r0 + mc] = y.T.astype(o_ref.dtype)
            else:
                o_ref[c, r0:r0 + mc, :] = y.astype(o_ref.dtype)


def qkv_proj(a, w, gain, col0, ncols, dh, norm, transposed, tm, tn, side=None):
    M, D = a.shape
    tm = min(tm, M)
    off = col0 // tn
    ni, nj = M // tm, ncols // tn
    if transposed:
        out_spec = pl.BlockSpec((tn // dh, dh, tm), lambda i, j: (j, 0, i))
        out_shape = jax.ShapeDtypeStruct((ncols // dh, dh, M), BF16)
    else:
        out_spec = pl.BlockSpec((tn // dh, tm, dh), lambda i, j: (j, i, 0))
        out_shape = jax.ShapeDtypeStruct((ncols // dh, M, dh), BF16)
    in_specs = [pl.BlockSpec((tm, D), lambda i, j: (i, 0)),
                pl.BlockSpec((D, tn), lambda i, j: (0, j + off)),
                pl.BlockSpec((1, tn), lambda i, j: (0, j + off))]
    args = [a, w, gain]
    side_rows = 0
    if side is not None:
        ada_w, ada_b3, st, layer, part, nparts = side
        side_rows = st.shape[0]
        ND = ada_w.shape[2]
        steps = ni * nj
        assert ND % (nparts * steps) == 0
        wcol = ND // (nparts * steps)
        assert wcol % LANE == 0
        blk0 = part * steps
        in_specs += [pl.BlockSpec((None, D, wcol), lambda i, j: (layer, 0, blk0 + i * nj + j)),
                     pl.BlockSpec((None, 1, wcol), lambda i, j: (layer, 0, blk0 + i * nj + j)),
                     pl.BlockSpec(st.shape, lambda i, j: (0, 0, 0), pipeline_mode=pl.Buffered(1))]
        args += [ada_w, ada_b3, st]
        out_spec = [out_spec, pl.BlockSpec((1, 8, wcol), lambda i, j: (0, 0, i * nj + j))]
        out_shape = [out_shape, jax.ShapeDtypeStruct((1, 8, steps * wcol), F32)]
    return pl.pallas_call(
        functools.partial(_qkv_kernel, norm=norm, transposed=transposed, dh=dh, side_rows=side_rows),
        grid=(ni, nj),
        in_specs=in_specs,
        out_specs=out_spec,
        out_shape=out_shape,
        compiler_params=_cp(("parallel", "arbitrary")),
        name="qkv",
    )(*args)


def _attn_bias_variants():
    out = []
    for var in range(3):
        tab = {}
        for i in range(ATTN_QROWS):
            for j in range(ATTN_KROWS):
                if var == 0:
                    valid, dr = j < NA_KH, j - i + NA_KH - 1
                elif var == 1:
                    valid, dr = i <= j < i + NA_KH, j - i + NA_KH // 2 - 1
                else:
                    valid, dr = j >= ATTN_KROWS - NA_KH, j - i - 1
                tab[(i, j)] = (valid, dr)
        out.append(tab)
    return out


def _attn_kernel(rpb_ref, q_ref, kt_ref, v_ref, kct_ref, vc_ref, o_ref, bias_ref, sl_ref, sc_ref,
                 pl_ref, pc_ref, den_ref, *, gw, rows):
    nblk = rows // ATTN_QROWS
    qn = ATTN_QROWS * gw
    kn = ATTN_KROWS * gw

    @pl.when(pl.program_id(1) == 0)
    def _build_bias():
        qc = lax.broadcasted_iota(I32, (gw, gw), 0)
        kc = lax.broadcasted_iota(I32, (gw, gw), 1)
        cs = jnp.clip(qc - NA_KW // 2, 0, gw - NA_KW)
        col_mask = jnp.where((kc >= cs) & (kc < cs + NA_KW), 0.0, NEG_INF).astype(F32)
        tiles = []
        for dr in range(2 * NA_KH - 1):
            r = jnp.broadcast_to(rpb_ref[0, dr:dr + 1, :], (gw, LANE))
            t = pltpu.roll(r, LANE - (NA_KW - 1), 1, stride=1, stride_axis=0)
            tiles.append(t[:, :gw] + col_mask)
        neg = jnp.full((gw, gw), NEG_INF, F32)
        for var, tab in enumerate(_attn_bias_variants()):
            for (i, j), (valid, dr) in tab.items():
                bias_ref[var, i * gw:(i + 1) * gw, j * gw:(j + 1) * gw] = tiles[dr] if valid else neg

    def geom(blk):
        if isinstance(blk, int):
            r0 = blk * ATTN_QROWS
            start = min(max(r0 - NA_KH // 2, 0), rows - ATTN_KROWS)
            var = 0 if blk == 0 else (2 if blk == nblk - 1 else 1)
            return r0 * gw, start * gw, var
        r0 = blk * ATTN_QROWS
        start = jnp.clip(r0 - NA_KH // 2, 0, rows - ATTN_KROWS)
        var = jnp.where(blk == 0, 0, jnp.where(blk == nblk - 1, 2, 1))
        return pl.multiple_of(r0 * gw, qn), pl.multiple_of(start * gw, LANE), var

    def scores(blk, slot):
        q0, k0, var = geom(blk)
        q = q_ref[pl.ds(q0, qn), :]
        sl_ref[slot] = _dot(q, kt_ref[:, pl.ds(k0, kn)]) + bias_ref[var]
        sc_ref[slot] = _dot(q, kct_ref[...])

    def softmax(slot):
        s_lat = sl_ref[slot]
        s_ctx = sc_ref[slot]
        m = jnp.maximum(jnp.max(s_lat, axis=-1, keepdims=True), jnp.max(s_ctx, axis=-1, keepdims=True))
        p_lat = jnp.exp(s_lat - m)
        p_ctx = jnp.exp(s_ctx - m)
        den_ref[slot] = jnp.sum(p_lat, axis=-1, keepdims=True) + jnp.sum(p_ctx, axis=-1, keepdims=True)
        pl_ref[slot] = p_lat.astype(BF16)
        pc_ref[slot] = p_ctx.astype(BF16)

    def values(blk, slot):
        q0, k0, _ = geom(blk)
        o = _dot(pl_ref[slot], v_ref[pl.ds(k0, kn), :]) + _dot(pc_ref[slot], vc_ref[...])
        o_ref[pl.ds(q0, qn), :] = (o * (1.0 / den_ref[slot])).astype(o_ref.dtype)

    scores(0, 0)
    scores(1, 1)
    softmax(0)

    def body(it, carry):
        i = 2 + 2 * it
        scores(i, 0)
        softmax(1)
        values(i - 2, 0)
        scores(i + 1, 1)
        softmax(0)
        values(i - 1, 1)
        return carry

    lax.fori_loop(0, (nblk - 2) // 2, body, 0)
    softmax(1)
    values(nblk - 2, 0)
    values(nblk - 1, 1)


def attention(q, kt, v, kct, vc, rpb_pad, cfg):
    B, N, D, NC, H, GW = cfg.B, cfg.N, cfg.D, cfg.NC, cfg.H, cfg.GW
    dh = D // H
    rows = N // GW
    assert dh == LANE and rows >= ATTN_KROWS and rows % (2 * ATTN_QROWS) == 0 and GW >= NA_KW
    assert (ATTN_QROWS * GW) % LANE == 0 and (NA_KH // 2 * GW) % LANE == 0 and ((rows - ATTN_KROWS) * GW) % LANE == 0
    return pl.pallas_call(
        functools.partial(_attn_kernel, gw=GW, rows=rows),
        grid=(H, B),
        in_specs=[pl.BlockSpec((1, 2 * NA_KH, LANE), lambda h, b: (h, 0, 0)),
                  pl.BlockSpec((None, N, dh), lambda h, b: (h, b, 0)),
                  pl.BlockSpec((None, dh, N), lambda h, b: (h, 0, b)),
                  pl.BlockSpec((None, N, dh), lambda h, b: (h, b, 0)),
                  pl.BlockSpec((None, dh, NC), lambda h, b: (h, 0, b)),
                  pl.BlockSpec((None, NC, dh), lambda h, b: (h, b, 0))],
        out_specs=pl.BlockSpec((N, dh), lambda h, b: (b, h)),
        out_shape=jax.ShapeDtypeStruct((B * N, D), BF16),
        scratch_shapes=[pltpu.VMEM((3, ATTN_QROWS * GW, ATTN_KROWS * GW), F32),
                        pltpu.VMEM((2, ATTN_QROWS * GW, ATTN_KROWS * GW), F32),
                        pltpu.VMEM((2, ATTN_QROWS * GW, NC), F32),
                        pltpu.VMEM((2, ATTN_QROWS * GW, ATTN_KROWS * GW), BF16),
                        pltpu.VMEM((2, ATTN_QROWS * GW, NC), BF16),
                        pltpu.VMEM((2, ATTN_QROWS * GW, 1), F32)],
        compiler_params=_cp(("arbitrary", "arbitrary")),
        name="attn",
    )(rpb_pad, q, kt, v, kct, vc)


def _proj_res_kernel(a_ref, w_ref, x_ref, g_ref, *rest, tiles_per_batch, scaled):
    if scaled:
        ps_ref, o_ref = rest
    else:
        (o_ref,) = rest
    wb = w_ref[0].astype(BF16)
    b = pl.program_id(0) // tiles_per_batch
    g = g_ref[0, pl.ds(b, 1), :]
    for r0, mc in _row_chunks(a_ref.shape[0]):
        y = _dot(a_ref[r0:r0 + mc, :], wb)
        if scaled:
            y = y * ps_ref[...]
        o_ref[r0:r0 + mc, :] = x_ref[r0:r0 + mc, :] + g * y


def proj_residual(a, w3, x2, mods, layer, gate_chunk, n_per_batch, pscale, tm, tn):
    M = a.shape[0]
    G, K, KO = w3.shape
    Dout = G * KO
    tm = min(tm, n_per_batch)
    tn = min(tn, KO)
    nj = KO // tn
    scaled = pscale is not None
    in_specs = [pl.BlockSpec((tm, K), lambda i, g, j: (i, g)),
                pl.BlockSpec((1, K, tn), lambda i, g, j: (g, 0, j)),
                pl.BlockSpec((tm, tn), lambda i, g, j: (i, g * nj + j)),
                pl.BlockSpec((1, 8, tn), lambda i, g, j: (layer, 0, gate_chunk * (Dout // tn) + g * nj + j))]
    args = [a, w3, x2, mods]
    if scaled:
        in_specs.append(pl.BlockSpec((1, tn), lambda i, g, j: (0, g * nj + j)))
        args.append(pscale)
    return pl.pallas_call(
        functools.partial(_proj_res_kernel, tiles_per_batch=n_per_batch // tm, scaled=scaled),
        grid=(M // tm, G, nj),
        in_specs=in_specs,
        out_specs=pl.BlockSpec((tm, tn), lambda i, g, j: (i, g * nj + j)),
        out_shape=jax.ShapeDtypeStruct((M, Dout), F32),
        compiler_params=_cp(("parallel", "arbitrary", "arbitrary")),
        name="proj_res",
    )(*args)


def _split_bf16(x):
    hi = x.astype(BF16)
    lo = (x - hi.astype(F32)).astype(BF16)
    return hi, lo


def _ffn_norm_kernel(x_ref, g_ref, sh_ref, sc_ref, wr_ref, hfx_ref, *, E):
    b = pl.program_id(0)
    D = x_ref.shape[2]
    hf = _norm_mod(x_ref[0], g_ref[0], sh_ref[0, pl.ds(b, 1), :], sc_ref[0, pl.ds(b, 1), :])
    bits = pltpu.bitcast(hf.astype(BF16).astype(F32), U32)
    hfx_ref[0, :, 0:D // 2] = (bits[:, 0:D // 2] >> 16) | bits[:, D // 2:D]
    h_hi, h_lo = _split_bf16(hf)
    w_hi, w_lo = _split_bf16(wr_ref[...])
    r = _dot(h_hi, jnp.concatenate([w_hi, w_lo], axis=1))
    lg = r[:, 0:LANE] + (r[:, LANE:2 * LANE] + _dot(h_lo, w_hi))
    lane = lax.broadcasted_iota(I32, (1, LANE), 1)
    lg = jnp.where(lane < E, lg, NEG_INF)
    ex = jnp.exp(lg - jnp.max(lg, axis=-1, keepdims=True))
    aff = ex / jnp.sum(ex, axis=-1, keepdims=True)
    hfx_ref[0, :, D // 2:D // 2 + LANE] = pltpu.bitcast(aff, U32)


def ffn_norm_router(x3, g3, mods, layer, wr_pad, E, tm):
    B, N, D = x3.shape
    tm = min(tm, N)
    return pl.pallas_call(
        functools.partial(_ffn_norm_kernel, E=E),
        grid=(B, N // tm),
        in_specs=[pl.BlockSpec((1, tm, D), lambda b, i: (b, i, 0)),
                  pl.BlockSpec((1, 1, D), lambda b, i: (layer, 0, 0)),
                  _mod_spec(D, layer, 3),
                  _mod_spec(D, layer, 4),
                  pl.BlockSpec((D, LANE), lambda b, i: (0, 0))],
        out_specs=pl.BlockSpec((1, tm, D // 2 + LANE), lambda b, i: (b, i, 0)),
        out_shape=jax.ShapeDtypeStruct((B, N, D // 2 + LANE), U32),
        compiler_params=_cp(("parallel", "parallel")),
        name="ffn_norm",
    )(x3, g3, mods, mods, wr_pad)


def _route_kernel(aff_ref, idx_ref, posq_ref, starts_ref, t_ref, cum_ref, *, E, cap, N):
    b = pl.program_id(0)
    nblk = N // TOK_TILE
    logn = int(math.log2(N))
    for k in range(N // LANE):
        t_ref[:, k * LANE:(k + 1) * LANE] = pltpu.bitcast(aff_ref[0, k * LANE:(k + 1) * LANE, :], F32).T
    bits = pltpu.bitcast(t_ref[0:E, :], I32)
    tok = lax.broadcasted_iota(I32, (1, N), 1)

    def count(pred):
        return jnp.sum(jnp.where(pred, 1, 0), axis=1, keepdims=True)

    def thr_step(i, thr):
        cand = thr | (jnp.int32(1) << (30 - i))
        return jnp.where(count(bits >= cand) >= cap, cand, thr)

    thr = lax.fori_loop(0, 31, thr_step, jnp.zeros((E, 1), I32))
    gt = bits > thr
    eq = bits == thr
    need = cap - count(gt)

    def tie_step(i, ans):
        cand = ans | (jnp.int32(1) << (logn - 1 - i))
        return jnp.where(count(eq & (tok < cand)) < need, cand, ans)

    last = lax.fori_loop(0, logn, tie_step, jnp.zeros((E, 1), I32))
    sel = gt | (eq & (tok <= last))
    sel_f = jnp.where(sel, 1.0, 0.0).astype(F32)

    ri = lax.broadcasted_iota(I32, (TOK_TILE, TOK_TILE), 0)
    ci = lax.broadcasted_iota(I32, (TOK_TILE, TOK_TILE), 1)
    tri = jnp.where(ri <= ci, 1.0, 0.0).astype(BF16)
    lane_s = lax.broadcasted_iota(I32, (1, LANE), 1)
    carry = jnp.zeros((E, 1), F32)
    starts = jnp.zeros((E, LANE), F32)
    for k in range(nblk):
        m = sel_f[:, k * TOK_TILE:(k + 1) * TOK_TILE]
        starts = jnp.where(lane_s == k, carry, starts)
        cum_ref[:, k * TOK_TILE:(k + 1) * TOK_TILE] = _dot(m.astype(BF16), tri) + carry
        carry = carry + jnp.sum(m, axis=1, keepdims=True)
    starts_ref[0] = jnp.where(lane_s == nblk, carry, starts).astype(I32)

    lane_c = lax.broadcasted_iota(I32, (1, cap), 1)

    def slot_step(c, out):
        cnt = jnp.sum(jnp.where(cum_ref[...] <= lax.convert_element_type(c, F32), 1, 0), axis=1, keepdims=True)
        return jnp.where(lane_c == c, cnt, out)

    idx = lax.fori_loop(0, cap, slot_step, jnp.zeros((E, cap), I32), unroll=4)
    idx_ref[0] = idx + b * N

    t_ref[0:E, :] = jnp.where(sel, cum_ref[...], 0.0)
    t_ref[E:LANE, :] = jnp.zeros((LANE - E, N), F32)
    for k in range(N // LANE):
        posq_ref[0, k * LANE:(k + 1) * LANE, :] = t_ref[:, k * LANE:(k + 1) * LANE].T


def route(hfx, cfg):
    B, N, D, E, cap = cfg.B, cfg.N, cfg.D, cfg.E, cfg.CAP
    nblk = N // TOK_TILE
    assert N % TOK_TILE == 0 and cap % LANE == 0 and (1 << int(math.log2(N))) == N
    assert E % 8 == 0 and nblk < LANE
    return pl.pallas_call(
        functools.partial(_route_kernel, E=E, cap=cap, N=N),
        grid=(B,),
        in_specs=[pl.BlockSpec((1, N, LANE), lambda b: (b, 0, D // 2 // LANE))],
        out_specs=[pl.BlockSpec((1, E, cap), lambda b: (b, 0, 0)),
                   pl.BlockSpec((1, N, LANE), lambda b: (b, 0, 0)),
                   pl.BlockSpec((1, E, LANE), lambda b: (b, 0, 0))],
        out_shape=[jax.ShapeDtypeStruct((B, E, cap), I32),
                   jax.ShapeDtypeStruct((B, N, LANE), F32),
                   jax.ShapeDtypeStruct((B, E, LANE), I32)],
        scratch_shapes=[pltpu.VMEM((LANE, N), F32), pltpu.VMEM((E, N), F32)],
        compiler_params=_cp(("arbitrary",)),
        name="route",
    )(hfx)


def _expert_kernel(idx_ref, hfx_hbm, wg_ref, wu_ref, wd_ref, o_ref, xs_ref, ga_ref, stg_ref, hid_ref,
                   sem, *, nf, tf, nn):
    e = pl.program_id(0)
    s = pl.program_id(1)
    ne = pl.num_programs(0)
    rows, D = xs_ref.shape
    rpt = rows // nn

    def row_copy(expert, r):
        tok = idx_ref[expert * rows + r]
        return pltpu.make_async_copy(hfx_hbm.at[pl.ds(tok, 1), :], stg_ref.at[pl.ds(r, 1), :], sem.at[0])

    @pl.when((e == 0) & (s == 0))
    def _():
        def body(r, carry):
            row_copy(0, r).start()
            return carry
        lax.fori_loop(0, rows, body, 0, unroll=8)

    @pl.when(s == 0)
    def _():
        pltpu.make_async_copy(hfx_hbm.at[pl.ds(0, rows), :], stg_ref, sem.at[0]).wait()
        for r0, mc in _row_chunks(rows):
            w = stg_ref[r0:r0 + mc, 0:D // 2]
            xs_ref[r0:r0 + mc, 0:D // 2] = pltpu.bitcast(w << 16, F32).astype(BF16)
            xs_ref[r0:r0 + mc, D // 2:D] = pltpu.bitcast(w & jnp.uint32(0xFFFF0000), F32).astype(BF16)
        ga_ref[...] = pltpu.bitcast(stg_ref[:, D // 2:D // 2 + LANE], F32)

    @pl.when(s < nf)
    def _():
        wg = wg_ref[0].astype(BF16)
        wu = wu_ref[0].astype(BF16)
        for r0, mc in _row_chunks(rows):
            x = xs_ref[r0:r0 + mc, :]
            hid_ref[s, r0:r0 + mc, :] = (_silu(_dot(x, wg)) * _dot(x, wu)).astype(BF16)

    @pl.when(s >= nf)
    def _():
        nxt = jnp.where(e + 1 < ne, e + 1, 0)
        base = (s - nf) * rpt
        for j in range(rpt):
            row_copy(nxt, base + j).start()
        wd = wd_ref[0].astype(BF16)
        lane = lax.broadcasted_iota(I32, (1, LANE), 1)
        for r0, mc in _row_chunks(rows):
            acc = _dot(hid_ref[0, r0:r0 + mc, :], wd[0:tf, :])
            for k in range(1, nf):
                acc = acc + _dot(hid_ref[k, r0:r0 + mc, :], wd[k * tf:(k + 1) * tf, :])
            gate = jnp.sum(jnp.where(lane == e, ga_ref[r0:r0 + mc, :], 0.0), axis=-1, keepdims=True)
            o_ref[r0:r0 + mc, :] = (acc * gate).astype(o_ref.dtype)

    @pl.when((e == ne - 1) & (s == pl.num_programs(1) - 1))
    def _():
        pltpu.make_async_copy(hfx_hbm.at[pl.ds(0, rows), :], stg_ref, sem.at[0]).wait()


def experts(idx_flat, hfx2, w_gate, w_up, w_down, layer, rows, tf, tn):
    _, E, D, F = w_gate.shape
    M = idx_flat.shape[0]
    DX = hfx2.shape[1]
    assert M == E * rows and DX == D // 2 + LANE
    tf = min(tf, F)
    tn = min(tn, D)
    nf, nn = F // tf, D // tn
    assert rows % nn == 0

    def w_in(e, s, idx):
        return (layer, e, 0, jnp.minimum(s, nf - 1))

    return pl.pallas_call(
        functools.partial(_expert_kernel, nf=nf, tf=tf, nn=nn),
        grid_spec=pltpu.PrefetchScalarGridSpec(
            num_scalar_prefetch=1,
            grid=(E, nf + nn),
            in_specs=[pl.BlockSpec(memory_space=pl.ANY),
                      pl.BlockSpec((None, 1, D, tf), w_in),
                      pl.BlockSpec((None, 1, D, tf), w_in),
                      pl.BlockSpec((None, 1, F, tn), lambda e, s, idx: (layer, e, 0, jnp.maximum(s - nf, 0)))],
            out_specs=pl.BlockSpec((rows, tn), lambda e, s, idx: (e, jnp.maximum(s - nf, 0))),
            scratch_shapes=[pltpu.VMEM((rows, D), BF16),
                            pltpu.VMEM((rows, LANE), F32),
                            pltpu.VMEM((rows, DX), U32),
                            pltpu.VMEM((nf, rows, tf), BF16),
                            pltpu.SemaphoreType.DMA((1,))]),
        out_shape=jax.ShapeDtypeStruct((M, D), BF16),
        compiler_params=_cp(("arbitrary", "arbitrary")),
        name="experts",
    )(idx_flat, hfx2, w_gate, w_up, w_down)


def _combine_kernel(st_ref, x_ref, g_ref, posq_ref, yg_hbm, o_ref, ssq_ref, ybuf, acc_ref, sem,
                    *, E, cap, B, nt):
    b = pl.program_id(0)
    t = pl.program_id(1)
    W = COMB_WIN
    KW = E * W
    step = b * nt + t
    slot = step % 2

    def tile_lo(bb, tt):
        base = (bb * (nt + 1) + tt) * E
        return [(st_ref[base + e] // BF16_ROWS) * BF16_ROWS for e in range(E)]

    def window_copy(bb, e, start, sl):
        row0 = pl.multiple_of((e * B + bb) * cap + start, BF16_ROWS)
        return pltpu.make_async_copy(yg_hbm.at[pl.ds(row0, W), :], ybuf.at[sl, pl.ds(e * W, W), :], sem.at[sl])

    def start_round0(bb, tt, sl):
        lo_n = tile_lo(bb, tt)
        for e in range(E):
            window_copy(bb, e, jnp.minimum(lo_n[e], cap - W), sl).start()

    @pl.when(step == 0)
    def _():
        start_round0(b, t, slot)

    @pl.when(step + 1 < B * nt)
    def _():
        start_round0((step + 1) // nt, (step + 1) % nt, 1 - slot)

    lo = tile_lo(b, t)
    base = (b * (nt + 1) + t) * E
    hi = [st_ref[base + E + e] for e in range(E)]
    nrounds = jnp.int32(1)
    for e in range(E):
        nrounds = jnp.maximum(nrounds, (hi[e] - lo[e] + (W - 1)) // W)

    qi = posq_ref[0].astype(I32)
    erow = lax.broadcasted_iota(I32, (LANE, KW), 0)
    ecol = lax.broadcasted_iota(I32, (LANE, KW), 1) // W
    expand = jnp.where(erow == ecol, 1.0, 0.0).astype(BF16)
    qe = (_dot((qi >> 5).astype(F32).astype(BF16), expand) * 32.0
          + _dot((qi & 31).astype(F32).astype(BF16), expand)).astype(I32) - 1
    lane_e = lax.broadcasted_iota(I32, (1, KW), 1) // W
    lane_k = lax.broadcasted_iota(I32, (1, KW), 1) % W

    def one_round(r, prefetched):
        lo_r = [lo[e] + r * W for e in range(E)]
        st = [jnp.minimum(lo_r[e], cap - W) for e in range(E)]
        if not prefetched:
            for e in range(E):
                window_copy(b, e, st[e], slot).start()
        stv = jnp.zeros((1, KW), I32)
        lov = jnp.zeros((1, KW), I32)
        for e in range(E):
            stv = jnp.where(lane_e == e, st[e], stv)
            lov = jnp.where(lane_e == e, lo_r[e], lov)
        onehot = (qe - stv == lane_k) & (qe >= lov) & (qe < lov + W)
        s = jnp.where(onehot, 1.0, 0.0).astype(BF16)
        for e in range(E):
            window_copy(b, e, st[e], slot).wait()
        return _dot(s, ybuf[slot])

    acc_ref[...] = one_round(0, True)

    def extra(r, carry):
        acc_ref[...] += one_round(r, False)
        return carry

    lax.fori_loop(1, nrounds, extra, 0)
    x2 = x_ref[...] + g_ref[0, pl.ds(b, 1), :] * acc_ref[...]
    o_ref[...] = x2
    ssq_ref[...] = jnp.sum(x2 * x2, axis=-1, keepdims=True)


def combine(starts_flat, x2d, mods, layer, posq, yg, cfg):
    B, N, D, E, cap = cfg.B, cfg.N, cfg.D, cfg.E, cfg.CAP
    nt = N // TOK_TILE
    assert cap >= COMB_WIN and cap % COMB_WIN == 0 and COMB_WIN % BF16_ROWS == 0 and cap <= 32 * 32
    return pl.pallas_call(
        functools.partial(_combine_kernel, E=E, cap=cap, B=B, nt=nt),
        grid_spec=pltpu.PrefetchScalarGridSpec(
            num_scalar_prefetch=1,
            grid=(B, nt),
            in_specs=[pl.BlockSpec((TOK_TILE, D), lambda b, t, st: (b * nt + t, 0)),
                      pl.BlockSpec((1, 8, D), lambda b, t, st: (layer, 0, 5)),
                      pl.BlockSpec((1, TOK_TILE, LANE), lambda b, t, st: (b, t, 0)),
                      pl.BlockSpec(memory_space=pl.ANY)],
            out_specs=[pl.BlockSpec((TOK_TILE, D), lambda b, t, st: (b * nt + t, 0)),
                       pl.BlockSpec((TOK_TILE, 1), lambda b, t, st: (b * nt + t, 0))],
            scratch_shapes=[pltpu.VMEM((2, E * COMB_WIN, D), BF16),
                            pltpu.VMEM((TOK_TILE, D), F32),
                            pltpu.SemaphoreType.DMA((2,))]),
        out_shape=[jax.ShapeDtypeStruct((B * N, D), F32),
                   jax.ShapeDtypeStruct((B * N, 1), F32)],
        compiler_params=_cp(("arbitrary", "arbitrary")),
        name="combine",
    )(starts_flat, x2d, mods, posq, yg)


def _pool_kernel(x_ref, ssq_ref, g_ref, sh_ref, sc_ref, o_ref, hs_ref, *, N, D, windows, chunks_per_group):
    b = pl.program_id(0)
    j = pl.program_id(1)
    P = max(windows) // 2
    cw = x_ref.shape[2]
    nblk = N // TOK_TILE
    g = g_ref[0]
    sh = sh_ref[0, pl.ds(b, 1), :]
    sc = sc_ref[0, pl.ds(b, 1), :]
    zeros = jnp.zeros((P, cw), F32)
    hs_ref[0:P, :] = zeros
    hs_ref[P + N:P + N + P, :] = zeros

    def fill(k, carry):
        r0 = pl.multiple_of(k * TOK_TILE, TOK_TILE)
        rstd = lax.rsqrt(ssq_ref[pl.ds(r0, TOK_TILE), :] * (1.0 / D) + NORM_EPS)
        y = x_ref[0, pl.ds(r0, TOK_TILE), :] * rstd * g
        hs_ref[pl.ds(pl.multiple_of(P + r0, 8), TOK_TILE), :] = y * (1.0 + sc) + sh
        return carry

    lax.fori_loop(0, nblk, fill, 0)

    for grp, w in enumerate(windows):
        half = w // 2

        @pl.when(j // chunks_per_group == grp)
        def _(half=half):
            def pool(k, carry):
                r0 = pl.multiple_of(k * TOK_TILE, TOK_TILE)
                big = hs_ref[pl.ds(r0, TOK_TILE + 2 * P), :]
                nrow = TOK_TILE + 2 * P
                fwd = big
                span = 1
                while span < 2 * half:
                    fwd = fwd + pltpu.roll(fwd, nrow - span, 0)
                    span *= 2
                acc = fwd[P - half:P - half + TOK_TILE]
                tk = lax.broadcasted_iota(I32, (TOK_TILE, 1), 0) + r0
                cnt = (jnp.minimum(tk + half, N) - jnp.maximum(tk - half, 0)).astype(F32)
                o_ref[0, pl.ds(r0, TOK_TILE), :] = (acc / cnt - big[P:P + TOK_TILE]).astype(o_ref.dtype)
                return carry

            lax.fori_loop(0, nblk, pool, 0)


def pool_mixer_input(x3, ssq, g3, mods, layer, cfg, cw):
    B, N, D = x3.shape
    dg = D // cfg.G
    cw = min(cw, dg)
    P = max(cfg.windows) // 2
    nj = D // cw

    def mspec(chunk):
        return pl.BlockSpec((1, 8, cw), lambda b, j: (layer, 0, chunk * nj + j))

    return pl.pallas_call(
        functools.partial(_pool_kernel, N=N, D=D, windows=cfg.windows, chunks_per_group=dg // cw),
        grid=(B, nj),
        in_specs=[pl.BlockSpec((1, N, cw), lambda b, j: (b, 0, j)),
                  pl.BlockSpec((N, 1), lambda b, j: (b, 0)),
                  pl.BlockSpec((1, 1, cw), lambda b, j: (layer, 0, j)),
                  mspec(0), mspec(1)],
        out_specs=pl.BlockSpec((1, N, cw), lambda b, j: (b, 0, j)),
        out_shape=jax.ShapeDtypeStruct((B, N, D), BF16),
        scratch_shapes=[pltpu.VMEM((N + 2 * P, cw), F32)],
        compiler_params=_cp(("parallel", "parallel")),
        name="pool",
    )(x3, ssq, g3, mods, mods)


def moe_block(x2d, mods, layer, g_ffn3, w_router, w_gate, w_up, w_down, cfg):
    B, N, D, E, cap = cfg.B, cfg.N, cfg.D, cfg.E, cfg.CAP
    nt = N // TOK_TILE
    wr_pad = jnp.pad(w_router, ((0, 0), (0, LANE - E)))
    hfx = ffn_norm_router(x2d.reshape(B, N, D), g_ffn3, mods, layer, wr_pad, E, tm=512)
    idx, posq, starts = route(hfx, cfg)
    idx_flat = jnp.transpose(idx, (1, 0, 2)).reshape(-1)
    starts_flat = jnp.transpose(starts[:, :, :nt + 1], (0, 2, 1)).reshape(-1)
    yg = experts(idx_flat, hfx.reshape(B * N, D // 2 + LANE), w_gate, w_up, w_down, layer, B * cap,
                 tf=256, tn=512)
    return combine(starts_flat, x2d, mods, layer, posq, yg, cfg)


def forward(cfg, x, c, ctx, c_ctx, ada_w, ada_b, norm_mix_g, norm_ffn_g, na_w_qkv, na_q_gain,
            na_k_gain, na_rpb, na_w_out, pool_w, pool_scale, moe_w_router, moe_w_gate,
            moe_w_up, moe_w_down):
    B, N, D, NC, H = cfg.B, cfg.N, cfg.D, cfg.NC, cfg.H
    dh = D // H
    L = ada_w.shape[0]
    assert L == 2 and B + 1 <= 8
    cond128 = jnp.concatenate([c, c_ctx[None, :], jnp.zeros((LANE - B - 1, D), F32)], axis=0)
    ada_b3 = ada_b.reshape(L, 1, ada_b.shape[1])
    mods0, st = ada_mods(cond128, ada_w, ada_b3, 0, B)
    gmix3 = norm_mix_g.reshape(L, 1, D)
    gffn3 = norm_ffn_g.reshape(L, 1, D)

    h = normmod(x, gmix3, mods0, 0, 0, None, tm=512).reshape(B * N, D)
    hc = normmod(ctx, gmix3, mods0, 0, 0, B, tm=512).reshape(B * NC, D)
    scale = 1.0 / math.sqrt(dh)
    gain = jnp.concatenate([jnp.tile(na_q_gain[0] * scale, H), jnp.tile(na_k_gain[0], H),
                            jnp.ones((D,), F32)])[None, :]
    w_qkv = na_w_qkv[0]
    q, m1a = qkv_proj(h, w_qkv, gain, 0, D, dh, True, False, tm=2048, tn=256, side=(ada_w, ada_b3, st, 1, 0, 3))
    kt, m1b = qkv_proj(h, w_qkv, gain, D, D, dh, True, True, tm=2048, tn=256, side=(ada_w, ada_b3, st, 1, 1, 3))
    v, m1c = qkv_proj(h, w_qkv, gain, 2 * D, D, dh, False, False, tm=2048, tn=256, side=(ada_w, ada_b3, st, 1, 2, 3))
    mods = jnp.concatenate([mods0, jnp.concatenate([m1a, m1b, m1c], axis=2)], axis=0)
    kct = qkv_proj(hc, w_qkv, gain, D, D, dh, True, True, tm=1024, tn=512)
    vc = qkv_proj(hc, w_qkv, gain, 2 * D, D, dh, False, False, tm=1024, tn=512)
    rpb_pad = jnp.pad(na_rpb[0], ((0, 0), (0, 1), (0, LANE - (2 * NA_KW - 1))))
    o = attention(q, kt, v, kct, vc, rpb_pad, cfg)
    x1 = proj_residual(o, na_w_out[0][None], x.reshape(B * N, D), mods, 0, 2, N, None, tm=2048, tn=256)
    x2, ssq = moe_block(x1, mods, 0, gffn3, moe_w_router[0], moe_w_gate, moe_w_up, moe_w_down, cfg)

    pooled = pool_mixer_input(x2.reshape(B, N, D), ssq, gmix3, mods, 1, cfg, cw=256).reshape(B * N, D)
    x3 = proj_residual(pooled, pool_w[0], x2, mods, 1, 2, N, pool_scale[0][None, :], tm=2048, tn=1024)
    x4, _ = moe_block(x3, mods, 1, gffn3, moe_w_router[1], moe_w_gate, moe_w_up, moe_w_down, cfg)
    return x4.reshape(B, N, D)


def kernel(x, c, ctx, c_ctx, ada_w, ada_b, norm_mix_g, norm_ffn_g, na_w_qkv, na_q_gain, na_k_gain, na_rpb, na_w_out, pool_w, pool_scale, moe_w_router, moe_w_gate, moe_w_up, moe_w_down):
    B, N, D = x.shape
    E, _, F = moe_w_gate.shape[1:]
    G = pool_w.shape[1]
    cfg = Cfg(B=B, N=N, D=D, NC=ctx.shape[1], H=na_rpb.shape[1], GW=64, E=E,
              CAP=max(1, 2 * N // E), F=F, G=G, windows=(2, 4, 8, 16))
    return forward(cfg, x, c, ctx, c_ctx, ada_w, ada_b, norm_mix_g, norm_ffn_g, na_w_qkv, na_q_gain,
                   na_k_gain, na_rpb, na_w_out, pool_w, pool_scale, moe_w_router, moe_w_gate,
                   moe_w_up, moe_w_down)
```
